```python
import jax, jax.numpy as jnp
from jax import lax
import numpy as np

D_MODEL = 1024
BATCH = 8
SEQ = 4096
DEPTH = 2

GRID_W = 64
CTX_LEN = 256
N_EVEN = (DEPTH + 1) // 2
N_ODD = DEPTH // 2
RMS_EPS = 1e-6

HEAD_DIM = 64
CONV_CH = D_MODEL // 2
CONV_K = 3
ATT_Q_HEADS = (D_MODEL // 2) // HEAD_DIM
ATT_KV_HEADS = ATT_Q_HEADS // 4
ATT_GROUP = ATT_Q_HEADS // ATT_KV_HEADS
ATT_Q_DIM = ATT_Q_HEADS * HEAD_DIM
ATT_KV_DIM = ATT_KV_HEADS * HEAD_DIM
IN_SPLITS = (CONV_CH, 2 * CONV_CH, 3 * CONV_CH, 3 * CONV_CH + ATT_Q_DIM, 3 * CONV_CH + ATT_Q_DIM + ATT_KV_DIM)
IN_PROJ_DIM = 3 * CONV_CH + ATT_Q_DIM + 2 * ATT_KV_DIM
Q_BLOCK = 128
ATT_SCALE = HEAD_DIM ** -0.5
ROPE_THETA = 10000.0
ROPE_AXIS_DIM = HEAD_DIM // 2

RWKV_HEAD = 64
RWKV_HEADS = D_MODEL // RWKV_HEAD
DECAY_LORA = 64
ICLR_LORA = 64
GATE_LORA = 128
GN_EPS = 64e-5
L2_EPS = 1e-12

PEER_HEADS = 8
PEER_NKEYS = 128
PEER_EXPERTS = PEER_NKEYS * PEER_NKEYS
PEER_TOPK = 16
PEER_QDIM = 256
PEER_HALF = PEER_QDIM // 2
PEER_CHUNK = 128

kernel_name = 'hybrid_conv_gqa_rwkv7_peer_dit'


def rms_norm(x, gain):
    xf = x.astype(jnp.float32)
    y = xf * lax.rsqrt(jnp.mean(xf * xf, axis=-1, keepdims=True) + RMS_EPS)
    return (y * gain.astype(jnp.float32)).astype(x.dtype)


def modulate(x, shift, scale):
    return x * (1 + scale) + shift


def rope_2d_tables(seq_len):
    t = jnp.arange(seq_len, dtype=jnp.int32)
    row = (t // GRID_W).astype(jnp.float32)
    col = (t % GRID_W).astype(jnp.float32)
    inv_freq = ROPE_THETA ** (-jnp.arange(0, ROPE_AXIS_DIM, 2, dtype=jnp.float32) / ROPE_AXIS_DIM)
    ang = jnp.concatenate([row[:, None] * inv_freq, col[:, None] * inv_freq], axis=-1)
    return jnp.cos(ang), jnp.sin(ang)


def apply_rope(x, cos, sin):
    shp = x.shape
    xf = x.astype(jnp.float32).reshape(shp[:-1] + (shp[-1] // 2, 2))
    x1, x2 = xf[..., 0], xf[..., 1]
    bshape = (cos.shape[0],) + (1,) * (x.ndim - 3) + (cos.shape[1],)
    c, s = cos.reshape(bshape), sin.reshape(bshape)
    out = jnp.stack([x1 * c - x2 * s, x1 * s + x2 * c], axis=-1)
    return out.reshape(shp).astype(x.dtype)


def short_conv(x, w):
    T = x.shape[1]
    pad = CONV_K // 2
    xp = jnp.pad(x, ((0, 0), (pad, pad), (0, 0)))
    return sum(xp[:, j:j + T] * w[j] for j in range(CONV_K))


def block_attention(q, k, v):
    B, T = q.shape[:2]
    nb = T // Q_BLOCK
    qb = jnp.moveaxis(q.reshape(B, nb, Q_BLOCK, ATT_KV_HEADS, ATT_GROUP, HEAD_DIM), 1, 0)

    def one_block(qblk):
        s = jnp.einsum('bqkgd,bskd->bkgqs', qblk, k, preferred_element_type=jnp.float32) * ATT_SCALE
        p = jax.nn.softmax(s, axis=-1).astype(v.dtype)
        return jnp.einsum('bkgqs,bskd->bqkgd', p, v)

    out = lax.map(one_block, qb)
    return jnp.moveaxis(out, 0, 1).reshape(B, T, ATT_Q_DIM)


def conv_attn_mixer(xn_c, xn_l, w_in, conv_w, q_gain, k_gain, w_out, rope_cos, rope_sin, need_ctx_out):
    B, S, _ = xn_l.shape
    L = xn_c.shape[1]
    h_l, gb_l, gc_l, q_l, k_l, v_l = jnp.split(xn_l @ w_in, IN_SPLITS, axis=-1)
    h_c, gb_c, gc_c, q_c, k_c, v_c = jnp.split(xn_c @ w_in, IN_SPLITS, axis=-1)
    k_l = apply_rope(rms_norm(k_l.reshape(B, S, ATT_KV_HEADS, HEAD_DIM), k_gain), rope_cos, rope_sin)
    k_c = rms_norm(k_c.reshape(B, L, ATT_KV_HEADS, HEAD_DIM), k_gain)
    v_l = v_l.reshape(B, S, ATT_KV_HEADS, HEAD_DIM)
    v_c = v_c.reshape(B, L, ATT_KV_HEADS, HEAD_DIM)
    q_l = apply_rope(rms_norm(q_l.reshape(B, S, ATT_Q_HEADS, HEAD_DIM), q_gain), rope_cos, rope_sin)
    att_l = block_attention(q_l.reshape(B, S, ATT_KV_HEADS, ATT_GROUP, HEAD_DIM),
                            jnp.concatenate([k_c, k_l], axis=1), jnp.concatenate([v_c, v_l], axis=1))
    conv_l = gb_l * short_conv(gc_l * h_l, conv_w)
    y_l = jnp.concatenate([conv_l, att_l], axis=-1) @ w_out
    y_c = None
    if need_ctx_out:
        q_c = rms_norm(q_c.reshape(B, L, ATT_Q_HEADS, HEAD_DIM), q_gain)
        att_c = block_attention(q_c.reshape(B, L, ATT_KV_HEADS, ATT_GROUP, HEAD_DIM), k_c, v_c)
        conv_c = gb_c * short_conv(gc_c * h_c, conv_w)
        y_c = jnp.concatenate([conv_c, att_c], axis=-1) @ w_out
    return y_l, y_c


def shift_latent(x):
    B, S, D = x.shape
    rows = S // GRID_W
    q = D // 4
    g = x.reshape(B, rows, GRID_W, D)
    left = jnp.pad(g[:, :, :-1, 0:q], ((0, 0), (0, 0), (1, 0), (0, 0)))
    right = jnp.pad(g[:, :, 1:, q:2 * q], ((0, 0), (0, 0), (0, 1), (0, 0)))
    up = jnp.pad(g[:, :-1, :, 2 * q:3 * q], ((0, 0), (1, 0), (0, 0), (0, 0)))
    down = jnp.pad(g[:, 1:, :, 3 * q:], ((0, 0), (0, 1), (0, 0), (0, 0)))
    return jnp.concatenate([left, right, up, down], axis=-1).reshape(B, S, D)


def shift_context(x):
    h = x.shape[-1] // 2
    prev = jnp.pad(x[:, :-1, :h], ((0, 0), (1, 0), (0, 0)))
    nxt = jnp.pad(x[:, 1:, h:], ((0, 0), (0, 1), (0, 0)))
    return jnp.concatenate([prev, nxt], axis=-1)


def _heads(t):
    return t.reshape(t.shape[0], t.shape[1], RWKV_HEADS, RWKV_HEAD)


def rwkv_project(xn, shifted, mu, w_r, w_k, w_v, g1, g2, k_k, k_a, w0, w1, w2, a0, a1, a2):
    f32 = jnp.float32
    xx = shifted - xn
    xr, xw, xk, xv, xa, xg = [xn + xx * mu[m] for m in range(6)]
    r = xr @ w_r
    k = xk @ w_k
    v = xv @ w_v
    g = jax.nn.sigmoid(xg @ g1) @ g2
    kk = _heads((k * k_k).astype(f32))
    kk = kk * lax.rsqrt(jnp.sum(kk * kk, axis=-1, keepdims=True) + L2_EPS)
    per_dir = []
    for d in range(2):
        logw = -jax.nn.softplus(-(w0[d] + jnp.tanh(xw @ w1[d]) @ w2[d]).astype(f32)) - 0.5
        decay = jnp.exp(-jnp.exp(logw))
        a = jax.nn.sigmoid((a0[d] + (xa @ a1[d]) @ a2[d]).astype(f32))
        kd = k.astype(f32) * (1 + (a - 1) * k_a)
        per_dir.append((_heads(decay), _heads(a), _heads(kd)))
    return _heads(r.astype(f32)), _heads(v.astype(f32)), g, kk, per_dir


def wkv_scan(r, decay, k, v, kk, a, state0, reverse, need_out):
    tm = lambda t: jnp.swapaxes(t, 0, 1)

    def step(S, inp):
        r_t, w_t, k_t, v_t, kk_t, a_t = inp
        sa = jnp.einsum('bhij,bhj->bhi', S, -kk_t)
        S = S * w_t[:, :, None, :] + sa[..., :, None] * (kk_t * a_t)[..., None, :] + v_t[..., :, None] * k_t[..., None, :]
        y = jnp.einsum('bhij,bhj->bhi', S, r_t) if need_out else None
        return S, y

    s_fin, ys = lax.scan(step, state0, (tm(r), tm(decay), tm(k), tm(v), tm(kk), tm(a)), reverse=reverse)
    return s_fin, (tm(ys) if need_out else None)


def rwkv7_bidir(xn_c, xn_l, mu, w_r, w_k, w_v, w_o, g1, g2, k_k, k_a, r_k, w0, w1, w2, a0, a1, a2, gn_w, gn_b, need_ctx_out):
    r_c, v_c, g_c, kk_c, dirs_c = rwkv_project(xn_c, shift_context(xn_c), mu, w_r, w_k, w_v, g1, g2, k_k, k_a, w0, w1, w2, a0, a1, a2)
    r_l, v_l, g_l, kk_l, dirs_l = rwkv_project(xn_l, shift_latent(xn_l), mu, w_r, w_k, w_v, g1, g2, k_k, k_a, w0, w1, w2, a0, a1, a2)
    B = xn_l.shape[0]
    state0 = jnp.zeros((B, RWKV_HEADS, RWKV_HEAD, RWKV_HEAD), jnp.float32)
    ys_l, ys_c = [], []
    for d, rev in enumerate((False, True)):
        dec_c, a_c, kd_c = dirs_c[d]
        s_ctx, y_cd = wkv_scan(r_c, dec_c, kd_c, v_c, kk_c, a_c, state0, rev, need_ctx_out)
        dec_l, a_l, kd_l = dirs_l[d]
        _, y_ld = wkv_scan(r_l, dec_l, kd_l, v_l, kk_l, a_l, s_ctx, rev, True)
        ys_l.append(y_ld)
        ys_c.append(y_cd)

    def readout(y_sum, r, v, g, k_sum):
        Bq, Tq = y_sum.shape[:2]
        mean = jnp.mean(y_sum, axis=-1, keepdims=True)
        var = jnp.mean(jnp.square(y_sum - mean), axis=-1, keepdims=True)
        yn = ((y_sum - mean) * lax.rsqrt(var + GN_EPS)).reshape(Bq, Tq, D_MODEL) * gn_w + gn_b
        bonus = (jnp.sum(r * k_sum * r_k, axis=-1, keepdims=True) * v).reshape(Bq, Tq, D_MODEL)
        return ((yn + bonus).astype(g.dtype) * g) @ w_o

    y_l = readout(ys_l[0] + ys_l[1], r_l, v_l, g_l, dirs_l[0][2] + dirs_l[1][2])
    y_c = None
    if need_ctx_out:
        y_c = readout(ys_c[0] + ys_c[1], r_c, v_c, g_c, dirs_c[0][2] + dirs_c[1][2])
    return y_l, y_c


def peer_ffn(xn, w_q, sub_keys, u_tab, v_tab):
    B, T, D = xn.shape
    chunks = xn.reshape(-1, PEER_CHUNK, D)

    def chunk_fn(xc):
        n = xc.shape[0]
        q = (xc @ w_q).reshape(n, PEER_HEADS, 2, PEER_HALF)
        s = jnp.einsum('nhpd,hpkd->nhpk', q, sub_keys, preferred_element_type=jnp.float32)
        s1, i1 = lax.top_k(s[:, :, 0], PEER_TOPK)
        s2, i2 = lax.top_k(s[:, :, 1], PEER_TOPK)
        cand_s = (s1[..., :, None] + s2[..., None, :]).reshape(n, PEER_HEADS, PEER_TOPK * PEER_TOPK)
        cand_i = (i1[..., :, None] * PEER_NKEYS + i2[..., None, :]).reshape(n, PEER_HEADS, PEER_TOPK * PEER_TOPK)
        top_s, pos = lax.top_k(cand_s, PEER_TOPK)
        idx = jnp.take_along_axis(cand_i, pos, axis=-1)
        gate = jax.nn.softmax(top_s, axis=-1)
        u = u_tab[idx]
        v = v_tab[idx]
        act = jax.nn.gelu(jnp.einsum('nd,nhkd->nhk', xc, u, preferred_element_type=jnp.float32), approximate=False)
        return jnp.einsum('nhk,nhkd->nd', (gate * act).astype(v.dtype), v)

    return lax.map(chunk_fn, chunks).reshape(B, T, D)


def setup_inputs(seed: int = 0) -> dict:
    key = jax.random.key(seed)
    ks = list(jax.random.split(key, 40))
    it = iter(ks)
    D = D_MODEL

    def nrm(shape, scale):
        return scale * jax.random.normal(next(it), shape, jnp.float32)

    inp = {}
    inp['x'] = nrm((BATCH, SEQ, D), 1.0)
    inp['c'] = nrm((BATCH, D), 1.0)
    inp['ctx'] = nrm((BATCH, CTX_LEN, D), 1.0)
    inp['c_ctx'] = nrm((D,), 1.0)
    inp['ada_w'] = nrm((DEPTH, D, 6 * D), 0.5 * D ** -0.5)
    inp['ada_b'] = nrm((DEPTH, 6 * D), 0.02)
    inp['norm1_g'] = 1.0 + nrm((DEPTH, D), 0.02)
    inp['norm2_g'] = 1.0 + nrm((DEPTH, D), 0.02)
    inp['ev_w_in'] = nrm((N_EVEN, D, IN_PROJ_DIM), D ** -0.5)
    inp['ev_conv_w'] = nrm((N_EVEN, CONV_K, CONV_CH), CONV_K ** -0.5)
    inp['ev_q_gain'] = 1.0 + nrm((N_EVEN, HEAD_DIM), 0.02)
    inp['ev_k_gain'] = 1.0 + nrm((N_EVEN, HEAD_DIM), 0.02)
    inp['ev_w_out'] = nrm((N_EVEN, D, D), D ** -0.5)
    inp['od_mu'] = jax.random.uniform(next(it), (N_ODD, 6, D), jnp.float32)
    inp['od_w_r'] = nrm((N_ODD, D, D), D ** -0.5)
    inp['od_w_k'] = nrm((N_ODD, D, D), D ** -0.5)
    inp['od_w_v'] = nrm((N_ODD, D, D), D ** -0.5)
    inp['od_w_o'] = nrm((N_ODD, D, D), D ** -0.5)
    inp['od_g1'] = nrm((N_ODD, D, GATE_LORA), D ** -0.5)
    inp['od_g2'] = nrm((N_ODD, GATE_LORA, D), GATE_LORA ** -0.5)
    inp['od_k_k'] = 0.85 + nrm((N_ODD, D), 0.02)
    inp['od_k_a'] = 1.0 + nrm((N_ODD, D), 0.02)
    inp['od_r_k'] = nrm((N_ODD, RWKV_HEADS, RWKV_HEAD), 0.1)
    inp['od_w0'] = -0.6 + nrm((N_ODD, 2, D), 0.3)
    inp['od_w1'] = nrm((N_ODD, 2, D, DECAY_LORA), D ** -0.5)
    inp['od_w2'] = nrm((N_ODD, 2, DECAY_LORA, D), 0.5 * DECAY_LORA ** -0.5)
    inp['od_a0'] = nrm((N_ODD, 2, D), 0.02)
    inp['od_a1'] = nrm((N_ODD, 2, D, ICLR_LORA), D ** -0.5)
    inp['od_a2'] = nrm((N_ODD, 2, ICLR_LORA, D), ICLR_LORA ** -0.5)
    inp['od_gn_w'] = 1.0 + nrm((N_ODD, D), 0.02)
    inp['od_gn_b'] = nrm((N_ODD, D), 0.02)
    inp['peer_wq'] = nrm((DEPTH, D, PEER_HEADS * PEER_QDIM), D ** -0.5)
    inp['peer_keys'] = nrm((DEPTH, PEER_HEADS, 2, PEER_NKEYS, PEER_HALF), PEER_HALF ** -0.5)
    inp['peer_u'] = nrm((DEPTH, PEER_EXPERTS, D), D ** -0.5)
    inp['peer_v'] = nrm((DEPTH, PEER_EXPERTS, D), PEER_HEADS ** -0.5)
    return inp


def reference(x, c, ctx, c_ctx, ada_w, ada_b, norm1_g, norm2_g, ev_w_in, ev_conv_w, ev_q_gain, ev_k_gain, ev_w_out,
              od_mu, od_w_r, od_w_k, od_w_v, od_w_o, od_g1, od_g2, od_k_k, od_k_a, od_r_k, od_w0, od_w1, od_w2,
              od_a0, od_a1, od_a2, od_gn_w, od_gn_b, peer_wq, peer_keys, peer_u, peer_v):
    rope_cos, rope_sin = rope_2d_tables(x.shape[1])
    silu_c = jax.nn.silu(c)
    silu_cc = jax.nn.silu(c_ctx)
    hx, hc = x, ctx
    for i in range(DEPTH):
        last = i == DEPTH - 1
        j = i // 2
        mod_l = jnp.split((silu_c @ ada_w[i] + ada_b[i])[:, None, :], 6, axis=-1)
        mod_c = jnp.split(silu_cc @ ada_w[i] + ada_b[i], 6, axis=-1)
        xn_l = modulate(rms_norm(hx, norm1_g[i]), mod_l[0], mod_l[1])
        xn_c = modulate(rms_norm(hc, norm1_g[i]), mod_c[0], mod_c[1])
        if i % 2 == 0:
            y_l, y_c = conv_attn_mixer(xn_c, xn_l, ev_w_in[j], ev_conv_w[j], ev_q_gain[j], ev_k_gain[j], ev_w_out[j],
                                       rope_cos, rope_sin, not last)
        else:
            y_l, y_c = rwkv7_bidir(xn_c, xn_l, od_mu[j], od_w_r[j], od_w_k[j], od_w_v[j], od_w_o[j], od_g1[j], od_g2[j],
                                   od_k_k[j], od_k_a[j], od_r_k[j], od_w0[j], od_w1[j], od_w2[j], od_a0[j], od_a1[j],
                                   od_a2[j], od_gn_w[j], od_gn_b[j], not last)
        hx = hx + mod_l[2] * y_l.astype(hx.dtype)
        hx = hx + mod_l[5] * peer_ffn(modulate(rms_norm(hx, norm2_g[i]), mod_l[3], mod_l[4]),
                                      peer_wq[i], peer_keys[i], peer_u[i], peer_v[i])
        if not last:
            hc = hc + mod_c[2] * y_c.astype(hc.dtype)
            hc = hc + mod_c[5] * peer_ffn(modulate(rms_norm(hc, norm2_g[i]), mod_c[3], mod_c[4]),
                                          peer_wq[i], peer_keys[i], peer_u[i], peer_v[i])
    return hx
```

```python
import functools
import math

import jax
import jax.numpy as jnp
from jax import lax
from jax.experimental import pallas as pl
from jax.experimental.pallas import tpu as pltpu

F32 = jnp.float32
BF16 = jnp.bfloat16

GRID_W = 64
RMS_EPS = 1e-6
HEAD_DIM = 64
CONV_CH = 512
ATT_Q_HEADS = 8
ATT_KV_HEADS = 2
ATT_GROUP = ATT_Q_HEADS // ATT_KV_HEADS
ATT_SCALE = HEAD_DIM ** -0.5
ROPE_THETA = 10000.0
RWKV_HEAD = 64
GN_EPS = 64e-5
L2_EPS = 1e-12
PEER_HEADS = 8
PEER_NKEYS = 128
PEER_TOPK = 16
PEER_HALF = 128

LANES = 128
SUBLANES = 8
ROW_TILE = 256
PROJ_TILE = 128
PEER_TOK_TILE = 256
PEER_EXP_CHUNK = 1024
SCAN_TB = 32
VMEM_LIMIT = 56 * 1024 * 1024


def _cp(sem, vmem=VMEM_LIMIT):
    return pltpu.CompilerParams(dimension_semantics=sem, vmem_limit_bytes=vmem)


def _full(shape):
    n = len(shape)
    return pl.BlockSpec(shape, lambda *_: (0,) * n)


def _pair_ones():
    r = lax.broadcasted_iota(jnp.int32, (LANES, LANES), 0) // HEAD_DIM
    c = lax.broadcasted_iota(jnp.int32, (LANES, LANES), 1) // HEAD_DIM
    return (r == c).astype(BF16)


def _segsum64(x, ones):
    hi = x.astype(BF16)
    lo = (x - hi.astype(F32)).astype(BF16)
    return (jnp.dot(hi, ones, preferred_element_type=F32)
            + jnp.dot(lo, ones, preferred_element_type=F32))


def _norm_mod(x, g, shift, scale):
    ms = jnp.mean(x * x, axis=-1, keepdims=True)
    return (x * lax.rsqrt(ms + RMS_EPS) * g) * (1.0 + scale) + shift


def _ada_kernel(c_ref, w_ref, b_ref, o_ref):
    c = c_ref[...]
    s = c * jax.nn.sigmoid(c)
    o_ref[...] = jnp.dot(s.astype(BF16), w_ref[...].astype(BF16), preferred_element_type=F32) + b_ref[...]


def ada_mods(cc, ada_w, ada_b):
    depth, d, d6 = ada_w.shape
    r = cc.shape[0]
    nblk = d6 // d
    return pl.pallas_call(
        _ada_kernel,
        grid=(depth, nblk),
        in_specs=[pl.BlockSpec((r, d), lambda i, j: (0, 0)),
                  pl.BlockSpec((None, d, d), lambda i, j: (i, 0, j)),
                  pl.BlockSpec((None, 1, d), lambda i, j: (i, 0, j))],
        out_specs=pl.BlockSpec((None, r, d), lambda i, j: (i, 0, j)),
        out_shape=jax.ShapeDtypeStruct((depth, r, d6), F32),
        compiler_params=_cp(("parallel", "parallel")),
        name="ada_mods",
    )(cc, ada_w, ada_b.reshape(depth, 1, d6))


def _inproj_kernel(h_ref, mods_ref, g_ref, w_ref, qg_ref, kg_ref, cos_ref, sin_ref,
                   u_ref, gb_ref, q_ref, k_ref, v_ref):
    ones = _pair_ones()
    x = h_ref[...]
    xn = _norm_mod(x, g_ref[...], mods_ref[0:1, :], mods_ref[1:2, :])
    y = jnp.dot(xn.astype(BF16), w_ref[...], preferred_element_type=F32)
    c = CONV_CH
    u_ref[...] = y[:, 2 * c:3 * c] * y[:, 0:c]
    gb_ref[...] = y[:, c:2 * c]
    cos = cos_ref[...]
    sin = sin_ref[...]
    lane = lax.broadcasted_iota(jnp.int32, (x.shape[0], LANES), 1)
    even = (lane % 2) == 0

    def qk_norm_rope(z, gain):
        ss = _segsum64(z * z, ones) * (1.0 / HEAD_DIM)
        zn = z * lax.rsqrt(ss + RMS_EPS) * gain
        partner = jnp.where(even, pltpu.roll(zn, LANES - 1, 1), pltpu.roll(zn, 1, 1))
        return zn * cos + partner * sin

    q0 = 3 * c
    for j in range(ATT_Q_HEADS * HEAD_DIM // LANES):
        z = y[:, q0 + j * LANES:q0 + (j + 1) * LANES]
        q_ref[:, j * LANES:(j + 1) * LANES] = qk_norm_rope(z, qg_ref[...]) * ATT_SCALE
    k0 = q0 + ATT_Q_HEADS * HEAD_DIM
    k_ref[...] = qk_norm_rope(y[:, k0:k0 + LANES], kg_ref[...]).astype(BF16)
    v_ref[...] = y[:, k0 + LANES:k0 + 2 * LANES].astype(BF16)


def in_proj(h, mods, g, w_in, q_gain, k_gain, cos_t, sin_t):
    b, t, d = h.shape
    tr = min(ROW_TILE, t)
    nt = t // tr
    proj = w_in.shape[1]
    row = lambda bi, si: (bi, si, 0)
    outs = pl.pallas_call(
        _inproj_kernel,
        grid=(b, nt),
        in_specs=[pl.BlockSpec((None, tr, d), row),
                  pl.BlockSpec((None, 6, d), lambda bi, si: (bi, 0, 0)),
                  _full((1, d)), _full((d, proj)), _full((1, LANES)), _full((1, LANES)),
                  pl.BlockSpec((tr, LANES), lambda bi, si: (si, 0)),
                  pl.BlockSpec((tr, LANES), lambda bi, si: (si, 0))],
        out_specs=[pl.BlockSpec((None, tr, CONV_CH), row),
                   pl.BlockSpec((None, tr, CONV_CH), row),
                   pl.BlockSpec((None, tr, CONV_CH), row),
                   pl.BlockSpec((None, tr, LANES), row),
                   pl.BlockSpec((None, tr, LANES), row)],
        out_shape=[jax.ShapeDtypeStruct((b, t, CONV_CH), F32),
                   jax.ShapeDtypeStruct((b, t, CONV_CH), F32),
                   jax.ShapeDtypeStruct((b, t, CONV_CH), F32),
                   jax.ShapeDtypeStruct((b, t, LANES), BF16),
                   jax.ShapeDtypeStruct((b, t, LANES), BF16)],
        compiler_params=_cp(("parallel", "parallel")),
        name="in_proj",
    )(h, mods, g.reshape(1, d), w_in, q_gain, k_gain, cos_t, sin_t)
    return outs


def _attn_kernel(*refs, n_sets):
    q_ref = refs[0]
    kv = refs[1:1 + 2 * n_sets]
    o_ref = refs[1 + 2 * n_sets]
    tq = q_ref.shape[0]
    lane = lax.broadcasted_iota(jnp.int32, (tq, LANES), 1)
    half = lane // HEAD_DIM
    nt = (((1,), (1,)), ((), ()))
    for cchunk in range(ATT_Q_HEADS * HEAD_DIM // LANES):
        qc = q_ref[:, cchunk * LANES:(cchunk + 1) * LANES]
        out_c = jnp.zeros((tq, LANES), F32)
        for hh in range(2):
            h = 2 * cchunk + hh
            j = h // ATT_GROUP
            qa = qc if hh == j else pltpu.roll(qc, HEAD_DIM, 1)
            qm = jnp.where(half == j, qa, 0.0).astype(BF16)
            ss = [lax.dot_general(qm, kv[2 * i][...], nt, preferred_element_type=F32)
                  for i in range(n_sets)]
            m = ss[0].max(axis=-1, keepdims=True)
            for s in ss[1:]:
                m = jnp.maximum(m, s.max(axis=-1, keepdims=True))
            l = jnp.zeros((tq, 1), F32)
            o = jnp.zeros((tq, LANES), F32)
            for i in range(n_sets):
                p = jnp.exp(ss[i] - m)
                l = l + p.sum(axis=-1, keepdims=True)
                o = o + jnp.dot(p.astype(BF16), kv[2 * i + 1][...], preferred_element_type=F32)
            o = o / l
            oa = o if hh == j else pltpu.roll(o, HEAD_DIM, 1)
            out_c = jnp.where(half == hh, oa, out_c)
        o_ref[:, cchunk * LANES:(cchunk + 1) * LANES] = out_c


def attention(q, kv_sets):
    b, t, qd = q.shape
    tq = min(ROW_TILE, t)
    in_specs = [pl.BlockSpec((None, tq, qd), lambda bi, si: (bi, si, 0))]
    args = [q]
    for k, v in kv_sets:
        ln = k.shape[1]
        in_specs += [pl.BlockSpec((None, ln, LANES), lambda bi, si: (bi, 0, 0))] * 2
        args += [k, v]
    return pl.pallas_call(
        functools.partial(_attn_kernel, n_sets=len(kv_sets)),
        grid=(b, t // tq),
        in_specs=in_specs,
        out_specs=pl.BlockSpec((None, tq, qd), lambda bi, si: (bi, si, 0)),
        out_shape=jax.ShapeDtypeStruct((b, t, qd), F32),
        compiler_params=_cp(("parallel", "parallel")),
        name="attention",
    )(*args)


def _conv_out_kernel(u_ref, up_ref, un_ref, gb_ref, att_ref, cw_ref, wo_ref, h_ref, mods_ref, o_ref):
    si = pl.program_id(1)
    last = pl.num_programs(1) - 1
    u = u_ref[...]
    tr = u.shape[0]
    row = lax.broadcasted_iota(jnp.int32, u.shape, 0)
    prev_row = jnp.where(si == 0, 0.0, up_ref[7:8, :])
    next_row = jnp.where(si == last, 0.0, un_ref[0:1, :])
    u_m1 = jnp.where(row == 0, prev_row, pltpu.roll(u, 1, 0))
    u_p1 = jnp.where(row == tr - 1, next_row, pltpu.roll(u, tr - 1, 0))
    conv = u_m1 * cw_ref[0:1, :] + u * cw_ref[1:2, :] + u_p1 * cw_ref[2:3, :]
    conv = gb_ref[...] * conv
    c = CONV_CH
    y = (jnp.dot(conv.astype(BF16), wo_ref[0:c, :], preferred_element_type=F32)
         + jnp.dot(att_ref[...].astype(BF16), wo_ref[c:2 * c, :], preferred_element_type=F32))
    o_ref[...] = h_ref[...] + mods_ref[2:3, :] * y


def conv_out(u, gb, att, conv_w, w_out, h, mods):
    b, t, d = h.shape
    tr = min(ROW_TILE, t)
    nt = t // tr
    r8 = tr // 8
    n8 = t // 8
    row = lambda bi, si: (bi, si, 0)
    return pl.pallas_call(
        _conv_out_kernel,
        grid=(b, nt),
        in_specs=[pl.BlockSpec((None, tr, CONV_CH), row),
                  pl.BlockSpec((None, 8, CONV_CH), lambda bi, si: (bi, jnp.maximum(si * r8 - 1, 0), 0)),
                  pl.BlockSpec((None, 8, CONV_CH), lambda bi, si: (bi, jnp.minimum((si + 1) * r8, n8 - 1), 0)),
                  pl.BlockSpec((None, tr, CONV_CH), row),
                  pl.BlockSpec((None, tr, CONV_CH), row),
                  _full((3, CONV_CH)), _full((d, d)),
                  pl.BlockSpec((None, tr, d), row),
                  pl.BlockSpec((None, 6, d), lambda bi, si: (bi, 0, 0))],
        out_specs=pl.BlockSpec((None, tr, d), row),
        out_shape=jax.ShapeDtypeStruct((b, t, d), F32),
        compiler_params=_cp(("parallel", "parallel")),
        name="conv_out",
    )(u, u, u, gb, att, conv_w, w_out, h, mods)


def _topk_rows(s, k, n):
    t = s.shape[1]
    iota = lax.broadcasted_iota(jnp.int32, (n, t), 0).astype(F32)
    krow = lax.broadcasted_iota(jnp.int32, (k, t), 0)
    vals = jnp.zeros((k, t), F32)
    idxs = jnp.zeros((k, t), F32)
    for it in range(k):
        m = jnp.max(s, axis=0, keepdims=True)
        idx = jnp.min(jnp.where(s == m, iota, float(n)), axis=0, keepdims=True)
        s = jnp.where(iota == idx, -jnp.inf, s)
        vals = jnp.where(krow == it, m, vals)
        idxs = jnp.where(krow == it, idx, idxs)
    return vals, idxs


def _router_kernel(h_ref, mods_ref, g_ref, wq_ref, keys_ref, xn_ref, a_ref, b_ref, gate_ref,
                   cand_ref, res_ref):
    x = h_ref[...]
    t = x.shape[0]
    xn = _norm_mod(x, g_ref[...], mods_ref[3:4, :], mods_ref[4:5, :])
    xb = xn.astype(BF16)
    xn_ref[...] = xb
    q = jnp.dot(xb, wq_ref[...], preferred_element_type=F32).astype(BF16)
    nt = (((1,), (1,)), ((), ()))
    kk = PEER_TOPK
    for h in range(PEER_HEADS):
        tops = []
        for p in range(2):
            g = 2 * h + p
            st = lax.dot_general(keys_ref[g], q[:, g * PEER_HALF:(g + 1) * PEER_HALF], nt,
                                 preferred_element_type=F32)
            tops.append(_topk_rows(st, kk, PEER_NKEYS))
        (s1, i1), (s2, i2) = tops
        for k1 in range(kk):
            cand_ref[k1 * kk:(k1 + 1) * kk, :] = s1[k1:k1 + 1, :] + s2
        top_s, pos = _topk_rows(cand_ref[...], kk, kk * kk)
        p1 = jnp.floor(pos * (1.0 / kk))
        p2 = pos - p1 * kk
        a_idx = jnp.zeros((kk, t), F32)
        b_idx = jnp.zeros((kk, t), F32)
        for k1 in range(kk):
            a_idx = a_idx + jnp.where(p1 == k1, i1[k1:k1 + 1, :], 0.0)
            b_idx = b_idx + jnp.where(p2 == k1, i2[k1:k1 + 1, :], 0.0)
        e = jnp.exp(top_s - top_s[0:1, :])
        gate = e / jnp.sum(e, axis=0, keepdims=True)
        res_ref[0, h * kk:(h + 1) * kk, :] = a_idx
        res_ref[1, h * kk:(h + 1) * kk, :] = b_idx
        res_ref[2, h * kk:(h + 1) * kk, :] = gate
    a_ref[...] = res_ref[0].T.astype(jnp.int32)
    b_ref[...] = res_ref[1].T.astype(jnp.int32)
    gate_ref[...] = res_ref[2].T


def peer_router(h, mods, g, wq, keys):
    b, t, d = h.shape
    tr = min(ROW_TILE, t)
    row = lambda bi, si: (bi, si, 0)
    hk = PEER_HEADS * PEER_TOPK
    return pl.pallas_call(
        _router_kernel,
        grid=(b, t // tr),
        in_specs=[pl.BlockSpec((None, tr, d), row),
                  pl.BlockSpec((None, 6, d), lambda bi, si: (bi, 0, 0)),
                  _full((1, d)), _full(wq.shape), _full(keys.shape)],
        out_specs=[pl.BlockSpec((None, tr, d), row),
                   pl.BlockSpec((None, tr, hk), row),
                   pl.BlockSpec((None, tr, hk), row),
                   pl.BlockSpec((None, tr, hk), row)],
        out_shape=[jax.ShapeDtypeStruct((b, t, d), BF16),
                   jax.ShapeDtypeStruct((b, t, hk), jnp.int32),
                   jax.ShapeDtypeStruct((b, t, hk), jnp.int32),
                   jax.ShapeDtypeStruct((b, t, hk), F32)],
        scratch_shapes=[pltpu.VMEM((PEER_TOPK * PEER_TOPK, tr), F32),
                        pltpu.VMEM((3, hk, tr), F32)],
        compiler_params=_cp(("parallel", "parallel")),
        name="peer_router",
    )(h, mods, g.reshape(1, d), wq, keys)


def _peer_dense_kernel(xn_ref, a_ref, b_ref, gate_ref, u_ref, v_ref, h_ref, mods_ref, o_ref,
                       g_ref, acc_ref):
    j = pl.program_id(1)
    tt = xn_ref.shape[0]
    nk = PEER_NKEYS
    nt = (((1,), (1,)), ((), ()))

    @pl.when(j == 0)
    def _():
        acc_ref[...] = jnp.zeros_like(acc_ref)
        sub = lax.broadcasted_iota(jnp.int32, (nk, a_ref.shape[1]), 0)

        def per_token(n, carry):
            ai = a_ref[pl.ds(n, 1), :]
            bi = b_ref[pl.ds(n, 1), :]
            gt = gate_ref[pl.ds(n, 1), :]
            at = jnp.where(sub == ai, gt, 0.0).astype(BF16)
            bt = (sub == bi).astype(BF16)
            g_ref[pl.ds(pl.multiple_of(n * nk, nk), nk), :] = lax.dot_general(
                at, bt, nt, preferred_element_type=F32)
            return carry

        lax.fori_loop(0, tt, per_token, 0)

    xb = xn_ref[...]
    hmat = lax.dot_general(xb, u_ref[...], nt, preferred_element_type=F32)
    nblk = u_ref.shape[0] // nk
    acc = acc_ref[...]
    for blk in range(nblk):
        hb = hmat[:, blk * nk:(blk + 1) * nk]
        act = 0.5 * hb * (1.0 + lax.erf(hb * (1.0 / math.sqrt(2.0))))
        gate_blk = g_ref[pl.ds(j * nblk + blk, tt, stride=nk), :]
        w = (gate_blk * act).astype(BF16)
        acc = acc + jnp.dot(w, v_ref[blk * nk:(blk + 1) * nk, :], preferred_element_type=F32)
    acc_ref[...] = acc

    @pl.when(j == pl.num_programs(1) - 1)
    def _():
        o_ref[...] = h_ref[...] + mods_ref[5:6, :] * acc_ref[...]


def peer_dense(xn, a_idx, b_idx, gate, u_tab, v_tab, h, mods):
    b, t, d = h.shape
    n = b * t
    tt = min(PEER_TOK_TILE, t)
    tiles_per_seq = t // tt
    ne = u_tab.shape[0]
    ec = PEER_EXP_CHUNK
    hk = a_idx.shape[-1]
    tok = lambda i, j: (i, 0)
    out = pl.pallas_call(
        _peer_dense_kernel,
        grid=(n // tt, ne // ec),
        in_specs=[pl.BlockSpec((tt, d), tok),
                  pl.BlockSpec((tt, hk), tok), pl.BlockSpec((tt, hk), tok), pl.BlockSpec((tt, hk), tok),
                  pl.BlockSpec((ec, d), lambda i, j: (j, 0)),
                  pl.BlockSpec((ec, d), lambda i, j: (j, 0)),
                  pl.BlockSpec((tt, d), tok),
                  pl.BlockSpec((None, 6, d), lambda i, j: (i // tiles_per_seq, 0, 0))],
        out_specs=pl.BlockSpec((tt, d), tok),
        out_shape=jax.ShapeDtypeStruct((n, d), F32),
        scratch_shapes=[pltpu.VMEM((tt * PEER_NKEYS, PEER_NKEYS), F32),
                        pltpu.VMEM((tt, d), F32)],
        compiler_params=_cp(("parallel", "arbitrary")),
        name="peer_dense",
    )(xn.reshape(n, d), a_idx.reshape(n, hk), b_idx.reshape(n, hk), gate.reshape(n, hk),
      u_tab, v_tab, h.reshape(n, d), mods)
    return out.reshape(b, t, d)


def peer_ffn_residual(h, mods, g, wq, keys, u_tab, v_tab):
    xn, a_idx, b_idx, gate = peer_router(h, mods, g, wq, keys)
    return peer_dense(xn, a_idx, b_idx, gate, u_tab, v_tab, h, mods)


def _normmod_kernel(h_ref, mods_ref, g_ref, o_ref):
    o_ref[...] = _norm_mod(h_ref[...], g_ref[...], mods_ref[0:1, :], mods_ref[1:2, :])


def norm_mod(h, mods, g):
    b, t, d = h.shape
    tr = min(ROW_TILE, t)
    row = lambda bi, si: (bi, si, 0)
    return pl.pallas_call(
        _normmod_kernel,
        grid=(b, t // tr),
        in_specs=[pl.BlockSpec((None, tr, d), row),
                  pl.BlockSpec((None, 6, d), lambda bi, si: (bi, 0, 0)),
                  _full((1, d))],
        out_specs=pl.BlockSpec((None, tr, d), row),
        out_shape=jax.ShapeDtypeStruct((b, t, d), F32),
        compiler_params=_cp(("parallel", "parallel")),
        name="norm_mod",
    )(h, mods, g.reshape(1, d))


def _rwkv_proj_kernel(x_ref, xp_ref, xnx_ref, mu_ref, wr_ref, wk_ref, wv_ref, g1_ref, g2_ref,
                      w1_ref, w2_ref, a1_ref, a2_ref, w0_ref, a0_ref, kk_ref, ka_ref,
                      *out_and_scratch, latent, need_rg):
    if need_rg:
        (r_ref, g_ref, v_ref, nkk_ref, dec0_ref, dec1_ref, kd0_ref, kd1_ref, bb0_ref, bb1_ref,
         ext_ref, sh_ref) = out_and_scratch
    else:
        (v_ref, nkk_ref, dec0_ref, dec1_ref, kd0_ref, kd1_ref, bb0_ref, bb1_ref,
         ext_ref, sh_ref) = out_and_scratch
    si = pl.program_id(1)
    last = pl.num_programs(1) - 1
    halo = xp_ref.shape[0]
    t, d = x_ref.shape
    x = x_ref[...]
    ext_ref[0:halo, :] = jnp.where(si == 0, 0.0, xp_ref[...])
    ext_ref[halo:halo + t, :] = x
    ext_ref[halo + t:halo + t + halo, :] = jnp.where(si == last, 0.0, xnx_ref[...])
    if latent:
        q = d // 4
        col = lax.broadcasted_iota(jnp.int32, (t, q), 0) % GRID_W
        sh_ref[:, 0:q] = jnp.where(col == 0, 0.0, ext_ref[halo - 1:halo - 1 + t, 0:q])
        sh_ref[:, q:2 * q] = jnp.where(col == GRID_W - 1, 0.0, ext_ref[halo + 1:halo + 1 + t, q:2 * q])
        sh_ref[:, 2 * q:3 * q] = ext_ref[halo - GRID_W:halo - GRID_W + t, 2 * q:3 * q]
        sh_ref[:, 3 * q:4 * q] = ext_ref[halo + GRID_W:halo + GRID_W + t, 3 * q:4 * q]
    else:
        hd = d // 2
        sh_ref[:, 0:hd] = ext_ref[halo - 1:halo - 1 + t, 0:hd]
        sh_ref[:, hd:d] = ext_ref[halo + 1:halo + 1 + t, hd:d]
    xx = sh_ref[...] - x

    def mix(m):
        return (x + xx * mu_ref[m:m + 1, :]).astype(BF16)

    ones = _pair_ones()
    lane = lax.broadcasted_iota(jnp.int32, (t, LANES), 1)
    k = jnp.dot(mix(2), wk_ref[...], preferred_element_type=F32)
    v_ref[...] = jnp.dot(mix(3), wv_ref[...], preferred_element_type=F32)
    if need_rg:
        r_ref[...] = jnp.dot(mix(0), wr_ref[...], preferred_element_type=F32)
        gg = jax.nn.sigmoid(jnp.dot(mix(5), g1_ref[...], preferred_element_type=F32))
        g_ref[...] = jnp.dot(gg.astype(BF16), g2_ref[...], preferred_element_type=F32)
    hw = jnp.tanh(jnp.dot(mix(1), w1_ref[...], preferred_element_type=F32))
    ha = jnp.dot(mix(4), a1_ref[...], preferred_element_type=F32)
    for c in range(d // LANES):
        sl = slice(c * LANES, (c + 1) * LANES)
        kc = k[:, sl] * kk_ref[:, sl]
        nrm = _segsum64(kc * kc, ones)
        kc = kc * lax.rsqrt(nrm + L2_EPS)
        nkk_ref[:, sl] = -kc
    kk = -nkk_ref[...]
    for dd, (dec_ref, kd_ref, bb_ref) in enumerate(((dec0_ref, kd0_ref, bb0_ref), (dec1_ref, kd1_ref, bb1_ref))):
        sel = (lane // (LANES // 2)) == dd
        z = w0_ref[dd:dd + 1, :] + jnp.dot(jnp.where(sel, hw, 0.0).astype(BF16), w2_ref[...],
                                            preferred_element_type=F32)
        nz = -z
        softplus = jnp.maximum(nz, 0.0) + jnp.log(1.0 + jnp.exp(-jnp.abs(nz)))
        logw = -softplus - 0.5
        dec_ref[...] = jnp.exp(-jnp.exp(logw))
        a = jax.nn.sigmoid(a0_ref[dd:dd + 1, :] + jnp.dot(jnp.where(sel, ha, 0.0).astype(BF16), a2_ref[...],
                                                          preferred_element_type=F32))
        kd_ref[...] = k * (1.0 + (a - 1.0) * ka_ref[...])
        bb_ref[...] = kk * a


def rwkv_proj(xn, wts, latent, need_rg):
    b, t, d = xn.shape
    tr = min(PROJ_TILE, t)
    nt = t // tr
    halo = GRID_W
    rh = tr // halo
    nh = t // halo
    row = lambda bi, si: (bi, si, 0)
    n_out = 10 if need_rg else 8
    w_specs = [_full(w.shape) for w in wts]
    outs = pl.pallas_call(
        functools.partial(_rwkv_proj_kernel, latent=latent, need_rg=need_rg),
        grid=(b, nt),
        in_specs=[pl.BlockSpec((None, tr, d), row),
                  pl.BlockSpec((None, halo, d), lambda bi, si: (bi, jnp.maximum(si * rh - 1, 0), 0)),
                  pl.BlockSpec((None, halo, d), lambda bi, si: (bi, jnp.minimum((si + 1) * rh, nh - 1), 0))]
                 + w_specs,
        out_specs=[pl.BlockSpec((None, tr, d), row)] * n_out,
        out_shape=[jax.ShapeDtypeStruct((b, t, d), F32)] * n_out,
        scratch_shapes=[pltpu.VMEM((tr + 2 * halo, d), F32), pltpu.VMEM((tr, d), F32)],
        compiler_params=_cp(("parallel", "parallel")),
        name="rwkv_proj",
    )(xn, xn, xn, *wts)
    return outs


def _scan_kernel(*refs, need_out, has_init, batch_group):
    it = iter(refs)
    dirs = []
    for _ in range(2):
        dirs.append([next(it) for _ in range(6 if need_out else 5)])
    s0_ref = next(it) if has_init else None
    y_refs = [next(it), next(it)] if need_out else None
    sfin_ref = next(it)
    s_ref = next(it)
    ti = pl.program_id(0)
    nb, tb, d = dirs[0][0].shape
    nchunk = d // LANES
    hd = RWKV_HEAD

    @pl.when(ti == 0)
    def _():
        if has_init:
            s_ref[...] = s0_ref[...]
        else:
            s_ref[...] = jnp.zeros_like(s_ref)

    ones = _pair_ones()
    eye = (lax.broadcasted_iota(jnp.int32, (hd, LANES), 1) % hd) == lax.broadcasted_iota(jnp.int32, (hd, LANES), 0)
    sub8 = lax.broadcasted_iota(jnp.int32, (SUBLANES, LANES), 0)

    def time_group(tg, carry):
        for dd in range(2):
            t8 = tg if dd == 0 else tb // SUBLANES - 1 - tg
            rows = pl.ds(pl.multiple_of(t8 * SUBLANES, SUBLANES), SUBLANES)
            refs_d = dirs[dd]

            def group(bg, carry2):
                for bi in range(batch_group):
                    bidx = bg * batch_group + bi
                    for c in range(nchunk):
                        sl = slice(c * LANES, (c + 1) * LANES)
                        tiles = [ref[bidx, rows, sl] for ref in refs_d]
                        s = s_ref[dd, bidx, c]
                        ytile = jnp.zeros((SUBLANES, LANES), F32)
                        for step in range(SUBLANES):
                            i = step if dd == 0 else SUBLANES - 1 - step
                            nkk_r, v_r, dec_r, kd_r, bb_r = [tl[i:i + 1, :] for tl in tiles[:5]]
                            sa = jnp.dot((s * nkk_r).astype(BF16), ones, preferred_element_type=F32)
                            vx = _segsum64(jnp.where(eye, v_r, 0.0), ones)
                            s = s * dec_r + sa * bb_r + vx * kd_r
                            if need_out:
                                yb = jnp.dot((s * tiles[5][i:i + 1, :]).astype(BF16), ones,
                                             preferred_element_type=F32)
                                yrow = jnp.sum(jnp.where(eye, yb, 0.0), axis=0, keepdims=True)
                                ytile = jnp.where(sub8 == i, yrow, ytile)
                        s_ref[dd, bidx, c] = s
                        if need_out:
                            y_refs[dd][bidx, rows, sl] = ytile
                return carry2

            lax.fori_loop(0, nb // batch_group, group, 0)
        return carry

    lax.fori_loop(0, tb // SUBLANES, time_group, 0)

    @pl.when(ti == pl.num_programs(0) - 1)
    def _():
        sfin_ref[...] = s_ref[...]


def wkv_scan(nkk, v, dec, kd, bb, r, s0):
    b, t, d = v.shape
    tb = min(SCAN_TB, t)
    nt = t // tb
    need_out = r is not None
    has_init = s0 is not None
    fwd = lambda i: (0, i, 0)
    bwd = lambda i: (0, nt - 1 - i, 0)
    args, in_specs = [], []
    for dd, imap in enumerate((fwd, bwd)):
        arrs = [nkk, v, dec[dd], kd[dd], bb[dd]] + ([r] if need_out else [])
        args += arrs
        in_specs += [pl.BlockSpec((b, tb, d), imap)] * len(arrs)
    st_shape = (2, b, d // LANES, RWKV_HEAD, LANES)
    if has_init:
        args.append(s0)
        in_specs.append(_full(st_shape))
    out_specs, out_shape = [], []
    if need_out:
        out_specs += [pl.BlockSpec((b, tb, d), fwd), pl.BlockSpec((b, tb, d), bwd)]
        out_shape += [jax.ShapeDtypeStruct((b, t, d), F32)] * 2
    out_specs.append(_full(st_shape))
    out_shape.append(jax.ShapeDtypeStruct(st_shape, F32))
    outs = pl.pallas_call(
        functools.partial(_scan_kernel, need_out=need_out, has_init=has_init,
                          batch_group=math.gcd(b, 2)),
        grid=(nt,),
        in_specs=in_specs,
        out_specs=out_specs,
        out_shape=out_shape,
        scratch_shapes=[pltpu.VMEM(st_shape, F32)],
        compiler_params=_cp(("arbitrary",)),
        name="wkv_scan",
    )(*args)
    if need_out:
        return outs[0], outs[1], outs[2]
    return None, None, outs[0]


def _readout_kernel(yf_ref, yb_ref, r_ref, v_ref, g_ref, kd0_ref, kd1_ref, rk_ref, gw_ref, gbias_ref,
                    wo_ref, h_ref, mods_ref, o_ref, z_ref):
    ones = _pair_ones()
    d = h_ref.shape[1]
    inv = 1.0 / RWKV_HEAD
    for c in range(d // LANES):
        sl = slice(c * LANES, (c + 1) * LANES)
        y = yf_ref[:, sl] + yb_ref[:, sl]
        mean = _segsum64(y, ones) * inv
        yc = y - mean
        var = _segsum64(yc * yc, ones) * inv
        yn = yc * lax.rsqrt(var + GN_EPS) * gw_ref[:, sl] + gbias_ref[:, sl]
        rkk = r_ref[:, sl] * (kd0_ref[:, sl] + kd1_ref[:, sl]) * rk_ref[:, sl]
        bonus = _segsum64(rkk, ones) * v_ref[:, sl]
        z_ref[:, sl] = ((yn + bonus) * g_ref[:, sl]).astype(BF16)
    out = jnp.dot(z_ref[...], wo_ref[...], preferred_element_type=F32)
    o_ref[...] = h_ref[...] + mods_ref[2:3, :] * out


def rwkv_readout(yf, yb, r, v, g, kd0, kd1, r_k, gn_w, gn_b, w_o, h, mods):
    b, t, d = h.shape
    tr = min(ROW_TILE, t)
    row = lambda bi, si: (bi, si, 0)
    rs = pl.BlockSpec((None, tr, d), row)
    return pl.pallas_call(
        _readout_kernel,
        grid=(b, t // tr),
        in_specs=[rs] * 7 + [_full((1, d))] * 3 + [_full((d, d)), rs,
                                                    pl.BlockSpec((None, 6, d), lambda bi, si: (bi, 0, 0))],
        out_specs=rs,
        out_shape=jax.ShapeDtypeStruct((b, t, d), F32),
        scratch_shapes=[pltpu.VMEM((tr, d), BF16)],
        compiler_params=_cp(("parallel", "parallel")),
        name="rwkv_readout",
    )(yf, yb, r, v, g, kd0, kd1, r_k.reshape(1, d), gn_w.reshape(1, d), gn_b.reshape(1, d), w_o, h, mods)


def _rope_tables(seq_len):
    t = jnp.arange(seq_len, dtype=jnp.int32)
    row = (t // GRID_W).astype(F32)
    col = (t % GRID_W).astype(F32)
    axis_dim = HEAD_DIM // 2
    inv_freq = ROPE_THETA ** (-jnp.arange(0, axis_dim, 2, dtype=F32) / axis_dim)
    ang = jnp.concatenate([row[:, None] * inv_freq, col[:, None] * inv_freq], axis=-1)
    cos = jnp.repeat(jnp.cos(ang), 2, axis=-1)
    sin = jnp.repeat(jnp.sin(ang), 2, axis=-1) * jnp.tile(jnp.array([-1.0, 1.0], F32), axis_dim)
    return jnp.tile(cos, (1, LANES // HEAD_DIM)), jnp.tile(sin, (1, LANES // HEAD_DIM))


def kernel(x, c, ctx, c_ctx, ada_w, ada_b, norm1_g, norm2_g, ev_w_in, ev_conv_w, ev_q_gain, ev_k_gain, ev_w_out,
           od_mu, od_w_r, od_w_k, od_w_v, od_w_o, od_g1, od_g2, od_k_k, od_k_a, od_r_k, od_w0, od_w1, od_w2,
           od_a0, od_a1, od_a2, od_gn_w, od_gn_b, peer_wq, peer_keys, peer_u, peer_v):
    b, s, d = x.shape
    lc = ctx.shape[1]
    depth = ada_w.shape[0]
    bf = lambda a: a.astype(BF16)

    pad_rows = (-(b + 1)) % 8
    cc = jnp.concatenate([c, c_ctx[None, :], jnp.zeros((pad_rows, d), F32)], axis=0)
    mods_all = ada_mods(cc, ada_w, ada_b)

    cos_l, sin_l = _rope_tables(s)
    cos_c = jnp.ones((lc, LANES), F32)
    sin_c = jnp.zeros((lc, LANES), F32)
    rep = LANES // HEAD_DIM

    hx, hc = x, ctx
    for i in range(depth):
        last = i == depth - 1
        j = i // 2
        mods_l = mods_all[i, :b].reshape(b, 6, d)
        mods_c = jnp.broadcast_to(mods_all[i, b].reshape(1, 6, d), (b, 6, d))
        if i % 2 == 0:
            w_in = bf(ev_w_in[j])
            w_out = bf(ev_w_out[j])
            qg = jnp.tile(ev_q_gain[j], rep).reshape(1, LANES)
            kg = jnp.tile(ev_k_gain[j], rep).reshape(1, LANES)
            u_l, gb_l, q_l, k_l, v_l = in_proj(hx, mods_l, norm1_g[i], w_in, qg, kg, cos_l, sin_l)
            u_c, gb_c, q_c, k_c, v_c = in_proj(hc, mods_c, norm1_g[i], w_in, qg, kg, cos_c, sin_c)
            att_l = attention(q_l, [(k_c, v_c), (k_l, v_l)])
            hx = conv_out(u_l, gb_l, att_l, ev_conv_w[j], w_out, hx, mods_l)
            if not last:
                att_c = attention(q_c, [(k_c, v_c)])
                hc = conv_out(u_c, gb_c, att_c, ev_conv_w[j], w_out, hc, mods_c)
        else:
            wts = [od_mu[j], bf(od_w_r[j]), bf(od_w_k[j]), bf(od_w_v[j]), bf(od_g1[j]), bf(od_g2[j]),
                   bf(jnp.concatenate([od_w1[j, 0], od_w1[j, 1]], axis=1)),
                   bf(jnp.concatenate([od_w2[j, 0], od_w2[j, 1]], axis=0)),
                   bf(jnp.concatenate([od_a1[j, 0], od_a1[j, 1]], axis=1)),
                   bf(jnp.concatenate([od_a2[j, 0], od_a2[j, 1]], axis=0)),
                   od_w0[j], od_a0[j], od_k_k[j].reshape(1, d), od_k_a[j].reshape(1, d)]
            xn_c = norm_mod(hc, mods_c, norm1_g[i])
            xn_l = norm_mod(hx, mods_l, norm1_g[i])
            pc = rwkv_proj(xn_c, wts, latent=False, need_rg=not last)
            pl_ = rwkv_proj(xn_l, wts, latent=True, need_rg=True)
            if last:
                v_c, nkk_c, d0c, d1c, kd0c, kd1c, bb0c, bb1c = pc
                r_c = g_c = None
            else:
                r_c, g_c, v_c, nkk_c, d0c, d1c, kd0c, kd1c, bb0c, bb1c = pc
            r_l, g_l, v_l, nkk_l, d0l, d1l, kd0l, kd1l, bb0l, bb1l = pl_
            yf_c, yb_c, s_ctx = wkv_scan(nkk_c, v_c, (d0c, d1c), (kd0c, kd1c), (bb0c, bb1c), r_c, None)
            yf_l, yb_l, _ = wkv_scan(nkk_l, v_l, (d0l, d1l), (kd0l, kd1l), (bb0l, bb1l), r_l, s_ctx)
            r_k = od_r_k[j].reshape(d)
            hx = rwkv_readout(yf_l, yb_l, r_l, v_l, g_l, kd0l, kd1l, r_k, od_gn_w[j], od_gn_b[j],
                              bf(od_w_o[j]), hx, mods_l)
            if not last:
                hc = rwkv_readout(yf_c, yb_c, r_c, v_c, g_c, kd0c, kd1c, r_k, od_gn_w[j], od_gn_b[j],
                                  bf(od_w_o[j]), hc, mods_c)
        wq = bf(peer_wq[i])
        keys = bf(peer_keys[i]).reshape(PEER_HEADS * 2, PEER_NKEYS, PEER_HALF)
        u_tab = bf(peer_u[i])
        v_tab = bf(peer_v[i])
        hx = peer_ffn_residual(hx, mods_l, norm2_g[i], wq, keys, u_tab, v_tab)
        if not last:
            hc = peer_ffn_residual(hc, mods_c, norm2_g[i], wq, keys, u_tab, v_tab)
    return hx
```

```python
import functools
import math

import jax
import jax.numpy as jnp
from jax import lax
from jax.experimental import pallas as pl
from jax.experimental.pallas import tpu as pltpu

F32 = jnp.float32
BF16 = jnp.bfloat16

GRID_W = 64
RMS_EPS = 1e-6
HEAD_DIM = 64
CONV_CH = 512
ATT_Q_HEADS = 8
ATT_KV_HEADS = 2
ATT_GROUP = ATT_Q_HEADS // ATT_KV_HEADS
ATT_SCALE = HEAD_DIM ** -0.5
ROPE_THETA = 10000.0
RWKV_HEAD = 64
GN_EPS = 64e-5
L2_EPS = 1e-12
PEER_HEADS = 8
PEER_NKEYS = 128
PEER_TOPK = 16
PEER_HALF = 128

LANES = 128
SUBLANES = 8
ROW_TILE = 256
PROJ_TILE = 128
PEER_TOK_TILE = 256
PEER_EXP_CHUNK = 1024
SCAN_CHUNK = 64
SCAN_BATCH_GROUP = 4
VMEM_LIMIT = 56 * 1024 * 1024


def _cp(sem, vmem=VMEM_LIMIT):
    return pltpu.CompilerParams(dimension_semantics=sem, vmem_limit_bytes=vmem)


def _full(shape):
    n = len(shape)
    return pl.BlockSpec(shape, lambda *_: (0,) * n)


def _pair_ones():
    r = lax.broadcasted_iota(jnp.int32, (LANES, LANES), 0) // HEAD_DIM
    c = lax.broadcasted_iota(jnp.int32, (LANES, LANES), 1) // HEAD_DIM
    return (r == c).astype(BF16)


def _segsum64(x, ones):
    hi = x.astype(BF16)
    lo = (x - hi.astype(F32)).astype(BF16)
    return (jnp.dot(hi, ones, preferred_element_type=F32)
            + jnp.dot(lo, ones, preferred_element_type=F32))


def _norm_mod(x, g, shift, scale):
    ms = jnp.mean(x * x, axis=-1, keepdims=True)
    return (x * lax.rsqrt(ms + RMS_EPS) * g) * (1.0 + scale) + shift


def _ada_kernel(c_ref, w_ref, b_ref, o_ref):
    c = c_ref[...]
    s = c * jax.nn.sigmoid(c)
    o_ref[...] = jnp.dot(s.astype(BF16), w_ref[...].astype(BF16), preferred_element_type=F32) + b_ref[...]


def ada_mods(cc, ada_w, ada_b):
    depth, d, d6 = ada_w.shape
    r = cc.shape[0]
    nblk = d6 // d
    return pl.pallas_call(
        _ada_kernel,
        grid=(depth, nblk),
        in_specs=[pl.BlockSpec((r, d), lambda i, j: (0, 0)),
                  pl.BlockSpec((None, d, d), lambda i, j: (i, 0, j)),
                  pl.BlockSpec((None, 1, d), lambda i, j: (i, 0, j))],
        out_specs=pl.BlockSpec((None, r, d), lambda i, j: (i, 0, j)),
        out_shape=jax.ShapeDtypeStruct((depth, r, d6), F32),
        compiler_params=_cp(("parallel", "parallel")),
        name="ada_mods",
    )(cc, ada_w, ada_b.reshape(depth, 1, d6))


def _inproj_kernel(h_ref, mods_ref, g_ref, w_ref, qg_ref, kg_ref, cos_ref, sin_ref,
                   u_ref, gb_ref, q_ref, k_ref, v_ref):
    ones = _pair_ones()
    x = h_ref[...]
    xn = _norm_mod(x, g_ref[...], mods_ref[0:1, :], mods_ref[1:2, :])
    y = jnp.dot(xn.astype(BF16), w_ref[...], preferred_element_type=F32)
    c = CONV_CH
    u_ref[...] = y[:, 2 * c:3 * c] * y[:, 0:c]
    gb_ref[...] = y[:, c:2 * c]
    cos = cos_ref[...]
    sin = sin_ref[...]
    lane = lax.broadcasted_iota(jnp.int32, (x.shape[0], LANES), 1)
    even = (lane % 2) == 0

    def qk_norm_rope(z, gain):
        ss = _segsum64(z * z, ones) * (1.0 / HEAD_DIM)
        zn = z * lax.rsqrt(ss + RMS_EPS) * gain
        partner = jnp.where(even, pltpu.roll(zn, LANES - 1, 1), pltpu.roll(zn, 1, 1))
        return zn * cos + partner * sin

    q0 = 3 * c
    for j in range(ATT_Q_HEADS * HEAD_DIM // LANES):
        z = y[:, q0 + j * LANES:q0 + (j + 1) * LANES]
        q_ref[:, j * LANES:(j + 1) * LANES] = qk_norm_rope(z, qg_ref[...]) * ATT_SCALE
    k0 = q0 + ATT_Q_HEADS * HEAD_DIM
    k_ref[...] = qk_norm_rope(y[:, k0:k0 + LANES], kg_ref[...]).astype(BF16)
    v_ref[...] = y[:, k0 + LANES:k0 + 2 * LANES].astype(BF16)


def in_proj(h, mods, g, w_in, q_gain, k_gain, cos_t, sin_t):
    b, t, d = h.shape
    tr = min(ROW_TILE, t)
    nt = t // tr
    proj = w_in.shape[1]
    row = lambda bi, si: (bi, si, 0)
    outs = pl.pallas_call(
        _inproj_kernel,
        grid=(b, nt),
        in_specs=[pl.BlockSpec((None, tr, d), row),
                  pl.BlockSpec((None, 6, d), lambda bi, si: (bi, 0, 0)),
                  _full((1, d)), _full((d, proj)), _full((1, LANES)), _full((1, LANES)),
                  pl.BlockSpec((tr, LANES), lambda bi, si: (si, 0)),
                  pl.BlockSpec((tr, LANES), lambda bi, si: (si, 0))],
        out_specs=[pl.BlockSpec((None, tr, CONV_CH), row),
                   pl.BlockSpec((None, tr, CONV_CH), row),
                   pl.BlockSpec((None, tr, CONV_CH), row),
                   pl.BlockSpec((None, tr, LANES), row),
                   pl.BlockSpec((None, tr, LANES), row)],
        out_shape=[jax.ShapeDtypeStruct((b, t, CONV_CH), F32),
                   jax.ShapeDtypeStruct((b, t, CONV_CH), F32),
                   jax.ShapeDtypeStruct((b, t, CONV_CH), F32),
                   jax.ShapeDtypeStruct((b, t, LANES), BF16),
                   jax.ShapeDtypeStruct((b, t, LANES), BF16)],
        compiler_params=_cp(("parallel", "parallel")),
        name="in_proj",
    )(h, mods, g.reshape(1, d), w_in, q_gain, k_gain, cos_t, sin_t)
    return outs


def _attn_kernel(*refs, n_sets):
    q_ref = refs[0]
    kv = refs[1:1 + 2 * n_sets]
    o_ref = refs[1 + 2 * n_sets]
    tq = q_ref.shape[0]
    lane = lax.broadcasted_iota(jnp.int32, (tq, LANES), 1)
    half = lane // HEAD_DIM
    nt = (((1,), (1,)), ((), ()))
    for cchunk in range(ATT_Q_HEADS * HEAD_DIM // LANES):
        qc = q_ref[:, cchunk * LANES:(cchunk + 1) * LANES]
        out_c = jnp.zeros((tq, LANES), F32)
        for hh in range(2):
            h = 2 * cchunk + hh
            j = h // ATT_GROUP
            qa = qc if hh == j else pltpu.roll(qc, HEAD_DIM, 1)
            qm = jnp.where(half == j, qa, 0.0).astype(BF16)
            ss = [lax.dot_general(qm, kv[2 * i][...], nt, preferred_element_type=F32)
                  for i in range(n_sets)]
            m = ss[0].max(axis=-1, keepdims=True)
            for s in ss[1:]:
                m = jnp.maximum(m, s.max(axis=-1, keepdims=True))
            l = jnp.zeros((tq, 1), F32)
            o = jnp.zeros((tq, LANES), F32)
            for i in range(n_sets):
                p = jnp.exp(ss[i] - m)
                l = l + p.sum(axis=-1, keepdims=True)
                o = o + jnp.dot(p.astype(BF16), kv[2 * i + 1][...], preferred_element_type=F32)
            o = o / l
            oa = o if hh == j else pltpu.roll(o, HEAD_DIM, 1)
            out_c = jnp.where(half == hh, oa, out_c)
        o_ref[:, cchunk * LANES:(cchunk + 1) * LANES] = out_c


def attention(q, kv_sets):
    b, t, qd = q.shape
    tq = min(ROW_TILE, t)
    in_specs = [pl.BlockSpec((None, tq, qd), lambda bi, si: (bi, si, 0))]
    args = [q]
    for k, v in kv_sets:
        ln = k.shape[1]
        in_specs += [pl.BlockSpec((None, ln, LANES), lambda bi, si: (bi, 0, 0))] * 2
        args += [k, v]
    return pl.pallas_call(
        functools.partial(_attn_kernel, n_sets=len(kv_sets)),
        grid=(b, t // tq),
        in_specs=in_specs,
        out_specs=pl.BlockSpec((None, tq, qd), lambda bi, si: (bi, si, 0)),
        out_shape=jax.ShapeDtypeStruct((b, t, qd), F32),
        compiler_params=_cp(("parallel", "parallel")),
        name="attention",
    )(*args)


def _conv_out_kernel(u_ref, up_ref, un_ref, gb_ref, att_ref, cw_ref, wo_ref, h_ref, mods_ref, o_ref):
    si = pl.program_id(1)
    last = pl.num_programs(1) - 1
    u = u_ref[...]
    tr = u.shape[0]
    row = lax.broadcasted_iota(jnp.int32, u.shape, 0)
    prev_row = jnp.where(si == 0, 0.0, up_ref[7:8, :])
    next_row = jnp.where(si == last, 0.0, un_ref[0:1, :])
    u_m1 = jnp.where(row == 0, prev_row, pltpu.roll(u, 1, 0))
    u_p1 = jnp.where(row == tr - 1, next_row, pltpu.roll(u, tr - 1, 0))
    conv = u_m1 * cw_ref[0:1, :] + u * cw_ref[1:2, :] + u_p1 * cw_ref[2:3, :]
    conv = gb_ref[...] * conv
    c = CONV_CH
    y = (jnp.dot(conv.astype(BF16), wo_ref[0:c, :], preferred_element_type=F32)
         + jnp.dot(att_ref[...].astype(BF16), wo_ref[c:2 * c, :], preferred_element_type=F32))
    o_ref[...] = h_ref[...] + mods_ref[2:3, :] * y


def conv_out(u, gb, att, conv_w, w_out, h, mods):
    b, t, d = h.shape
    tr = min(ROW_TILE, t)
    nt = t // tr
    r8 = tr // 8
    n8 = t // 8
    row = lambda bi, si: (bi, si, 0)
    return pl.pallas_call(
        _conv_out_kernel,
        grid=(b, nt),
        in_specs=[pl.BlockSpec((None, tr, CONV_CH), row),
                  pl.BlockSpec((None, 8, CONV_CH), lambda bi, si: (bi, jnp.maximum(si * r8 - 1, 0), 0)),
                  pl.BlockSpec((None, 8, CONV_CH), lambda bi, si: (bi, jnp.minimum((si + 1) * r8, n8 - 1), 0)),
                  pl.BlockSpec((None, tr, CONV_CH), row),
                  pl.BlockSpec((None, tr, CONV_CH), row),
                  _full((3, CONV_CH)), _full((d, d)),
                  pl.BlockSpec((None, tr, d), row),
                  pl.BlockSpec((None, 6, d), lambda bi, si: (bi, 0, 0))],
        out_specs=pl.BlockSpec((None, tr, d), row),
        out_shape=jax.ShapeDtypeStruct((b, t, d), F32),
        compiler_params=_cp(("parallel", "parallel")),
        name="conv_out",
    )(u, u, u, gb, att, conv_w, w_out, h, mods)


def _topk_rows(s, k, n):
    t = s.shape[1]
    iota = lax.broadcasted_iota(jnp.int32, (n, t), 0).astype(F32)
    krow = lax.broadcasted_iota(jnp.int32, (k, t), 0)
    vals = jnp.zeros((k, t), F32)
    idxs = jnp.zeros((k, t), F32)
    for it in range(k):
        m = jnp.max(s, axis=0, keepdims=True)
        idx = jnp.min(jnp.where(s == m, iota, float(n)), axis=0, keepdims=True)
        s = jnp.where(iota == idx, -jnp.inf, s)
        vals = jnp.where(krow == it, m, vals)
        idxs = jnp.where(krow == it, idx, idxs)
    return vals, idxs


def _router_kernel(h_ref, mods_ref, g_ref, wq_ref, keys_ref, xn_ref, a_ref, b_ref, gate_ref,
                   cand_ref, res_ref):
    x = h_ref[...]
    t = x.shape[0]
    xn = _norm_mod(x, g_ref[...], mods_ref[3:4, :], mods_ref[4:5, :])
    xb = xn.astype(BF16)
    xn_ref[...] = xb
    q = jnp.dot(xb, wq_ref[...], preferred_element_type=F32).astype(BF16)
    nt = (((1,), (1,)), ((), ()))
    kk = PEER_TOPK
    for h in range(PEER_HEADS):
        tops = []
        for p in range(2):
            g = 2 * h + p
            st = lax.dot_general(keys_ref[g], q[:, g * PEER_HALF:(g + 1) * PEER_HALF], nt,
                                 preferred_element_type=F32)
            tops.append(_topk_rows(st, kk, PEER_NKEYS))
        (s1, i1), (s2, i2) = tops
        for k1 in range(kk):
            cand_ref[k1 * kk:(k1 + 1) * kk, :] = s1[k1:k1 + 1, :] + s2
        top_s, pos = _topk_rows(cand_ref[...], kk, kk * kk)
        p1 = jnp.floor(pos * (1.0 / kk))
        p2 = pos - p1 * kk
        a_idx = jnp.zeros((kk, t), F32)
        b_idx = jnp.zeros((kk, t), F32)
        for k1 in range(kk):
            a_idx = a_idx + jnp.where(p1 == k1, i1[k1:k1 + 1, :], 0.0)
            b_idx = b_idx + jnp.where(p2 == k1, i2[k1:k1 + 1, :], 0.0)
        e = jnp.exp(top_s - top_s[0:1, :])
        gate = e / jnp.sum(e, axis=0, keepdims=True)
        res_ref[0, h * kk:(h + 1) * kk, :] = a_idx
        res_ref[1, h * kk:(h + 1) * kk, :] = b_idx
        res_ref[2, h * kk:(h + 1) * kk, :] = gate
    a_ref[...] = res_ref[0].T.astype(jnp.int32)
    b_ref[...] = res_ref[1].T.astype(jnp.int32)
    gate_ref[...] = res_ref[2].T


def peer_router(h, mods, g, wq, keys):
    b, t, d = h.shape
    tr = min(ROW_TILE, t)
    row = lambda bi, si: (bi, si, 0)
    hk = PEER_HEADS * PEER_TOPK
    return pl.pallas_call(
        _router_kernel,
        grid=(b, t // tr),
        in_specs=[pl.BlockSpec((None, tr, d), row),
                  pl.BlockSpec((None, 6, d), lambda bi, si: (bi, 0, 0)),
                  _full((1, d)), _full(wq.shape), _full(keys.shape)],
        out_specs=[pl.BlockSpec((None, tr, d), row),
                   pl.BlockSpec((None, tr, hk), row),
                   pl.BlockSpec((None, tr, hk), row),
                   pl.BlockSpec((None, tr, hk), row)],
        out_shape=[jax.ShapeDtypeStruct((b, t, d), BF16),
                   jax.ShapeDtypeStruct((b, t, hk), jnp.int32),
                   jax.ShapeDtypeStruct((b, t, hk), jnp.int32),
                   jax.ShapeDtypeStruct((b, t, hk), F32)],
        scratch_shapes=[pltpu.VMEM((PEER_TOPK * PEER_TOPK, tr), F32),
                        pltpu.VMEM((3, hk, tr), F32)],
        compiler_params=_cp(("parallel", "parallel")),
        name="peer_router",
    )(h, mods, g.reshape(1, d), wq, keys)


def _peer_dense_kernel(xn_ref, a_ref, b_ref, gate_ref, u_ref, v_ref, h_ref, mods_ref, o_ref,
                       g_ref, acc_ref):
    j = pl.program_id(1)
    tt = xn_ref.shape[0]
    nk = PEER_NKEYS
    nt = (((1,), (1,)), ((), ()))

    @pl.when(j == 0)
    def _():
        acc_ref[...] = jnp.zeros_like(acc_ref)
        sub = lax.broadcasted_iota(jnp.int32, (nk, a_ref.shape[1]), 0)

        def per_group(i, carry):
            rows = pl.ds(pl.multiple_of(i * SUBLANES, SUBLANES), SUBLANES)
            ai8, bi8, gt8 = a_ref[rows, :], b_ref[rows, :], gate_ref[rows, :]
            for k in range(SUBLANES):
                at = jnp.where(sub == ai8[k:k + 1, :], gt8[k:k + 1, :], 0.0).astype(BF16)
                bt = (sub == bi8[k:k + 1, :]).astype(BF16)
                g_ref[pl.ds(pl.multiple_of((i * SUBLANES + k) * nk, nk), nk), :] = lax.dot_general(
                    at, bt, nt, preferred_element_type=F32)
            return carry

        lax.fori_loop(0, tt // SUBLANES, per_group, 0)

    xb = xn_ref[...]
    hmat = lax.dot_general(xb, u_ref[...], nt, preferred_element_type=F32)
    nblk = u_ref.shape[0] // nk
    acc = acc_ref[...]
    for blk in range(nblk):
        hb = hmat[:, blk * nk:(blk + 1) * nk]
        act = 0.5 * hb * (1.0 + lax.erf(hb * (1.0 / math.sqrt(2.0))))
        gate_blk = g_ref[pl.ds(j * nblk + blk, tt, stride=nk), :]
        w = (gate_blk * act).astype(BF16)
        acc = acc + jnp.dot(w, v_ref[blk * nk:(blk + 1) * nk, :], preferred_element_type=F32)
    acc_ref[...] = acc

    @pl.when(j == pl.num_programs(1) - 1)
    def _():
        o_ref[...] = h_ref[...] + mods_ref[5:6, :] * acc_ref[...]


def peer_dense(xn, a_idx, b_idx, gate, u_tab, v_tab, h, mods):
    b, t, d = h.shape
    n = b * t
    tt = min(PEER_TOK_TILE, t)
    tiles_per_seq = t // tt
    ne = u_tab.shape[0]
    ec = PEER_EXP_CHUNK
    hk = a_idx.shape[-1]
    tok = lambda i, j: (i, 0)
    out = pl.pallas_call(
        _peer_dense_kernel,
        grid=(n // tt, ne // ec),
        in_specs=[pl.BlockSpec((tt, d), tok),
                  pl.BlockSpec((tt, hk), tok), pl.BlockSpec((tt, hk), tok), pl.BlockSpec((tt, hk), tok),
                  pl.BlockSpec((ec, d), lambda i, j: (j, 0)),
                  pl.BlockSpec((ec, d), lambda i, j: (j, 0)),
                  pl.BlockSpec((tt, d), tok),
                  pl.BlockSpec((None, 6, d), lambda i, j: (i // tiles_per_seq, 0, 0))],
        out_specs=pl.BlockSpec((tt, d), tok),
        out_shape=jax.ShapeDtypeStruct((n, d), F32),
        scratch_shapes=[pltpu.VMEM((tt * PEER_NKEYS, PEER_NKEYS), F32),
                        pltpu.VMEM((tt, d), F32)],
        compiler_params=_cp(("parallel", "arbitrary")),
        name="peer_dense",
    )(xn.reshape(n, d), a_idx.reshape(n, hk), b_idx.reshape(n, hk), gate.reshape(n, hk),
      u_tab, v_tab, h.reshape(n, d), mods)
    return out.reshape(b, t, d)


def peer_ffn_residual(h, mods, g, wq, keys, u_tab, v_tab):
    xn, a_idx, b_idx, gate = peer_router(h, mods, g, wq, keys)
    return peer_dense(xn, a_idx, b_idx, gate, u_tab, v_tab, h, mods)


def _normmod_kernel(h_ref, mods_ref, g_ref, o_ref):
    o_ref[...] = _norm_mod(h_ref[...], g_ref[...], mods_ref[0:1, :], mods_ref[1:2, :])


def norm_mod(h, mods, g):
    b, t, d = h.shape
    tr = min(ROW_TILE, t)
    row = lambda bi, si: (bi, si, 0)
    return pl.pallas_call(
        _normmod_kernel,
        grid=(b, t // tr),
        in_specs=[pl.BlockSpec((None, tr, d), row),
                  pl.BlockSpec((None, 6, d), lambda bi, si: (bi, 0, 0)),
                  _full((1, d))],
        out_specs=pl.BlockSpec((None, tr, d), row),
        out_shape=jax.ShapeDtypeStruct((b, t, d), F32),
        compiler_params=_cp(("parallel", "parallel")),
        name="norm_mod",
    )(h, mods, g.reshape(1, d))


def _rwkv_proj_kernel(x_ref, xp_ref, xnx_ref, mu_ref, wr_ref, wk_ref, wv_ref, g1_ref, g2_ref,
                      w1_ref, w2_ref, a1_ref, a2_ref, w0_ref, a0_ref, kk_ref, ka_ref,
                      *out_and_scratch, latent, need_rg):
    if need_rg:
        (r_ref, g_ref, v_ref, nkk_ref, dec0_ref, dec1_ref, kd0_ref, kd1_ref, bb0_ref, bb1_ref,
         ext_ref, sh_ref) = out_and_scratch
    else:
        (v_ref, nkk_ref, dec0_ref, dec1_ref, kd0_ref, kd1_ref, bb0_ref, bb1_ref,
         ext_ref, sh_ref) = out_and_scratch
    si = pl.program_id(1)
    last = pl.num_programs(1) - 1
    halo = xp_ref.shape[0]
    t, d = x_ref.shape
    x = x_ref[...]
    ext_ref[0:halo, :] = jnp.where(si == 0, 0.0, xp_ref[...])
    ext_ref[halo:halo + t, :] = x
    ext_ref[halo + t:halo + t + halo, :] = jnp.where(si == last, 0.0, xnx_ref[...])
    if latent:
        q = d // 4
        col = lax.broadcasted_iota(jnp.int32, (t, q), 0) % GRID_W
        sh_ref[:, 0:q] = jnp.where(col == 0, 0.0, ext_ref[halo - 1:halo - 1 + t, 0:q])
        sh_ref[:, q:2 * q] = jnp.where(col == GRID_W - 1, 0.0, ext_ref[halo + 1:halo + 1 + t, q:2 * q])
        sh_ref[:, 2 * q:3 * q] = ext_ref[halo - GRID_W:halo - GRID_W + t, 2 * q:3 * q]
        sh_ref[:, 3 * q:4 * q] = ext_ref[halo + GRID_W:halo + GRID_W + t, 3 * q:4 * q]
    else:
        hd = d // 2
        sh_ref[:, 0:hd] = ext_ref[halo - 1:halo - 1 + t, 0:hd]
        sh_ref[:, hd:d] = ext_ref[halo + 1:halo + 1 + t, hd:d]
    xx = sh_ref[...] - x

    def mix(m):
        return (x + xx * mu_ref[m:m + 1, :]).astype(BF16)

    ones = _pair_ones()
    lane = lax.broadcasted_iota(jnp.int32, (t, LANES), 1)
    k = jnp.dot(mix(2), wk_ref[...], preferred_element_type=F32)
    v_ref[...] = jnp.dot(mix(3), wv_ref[...], preferred_element_type=F32)
    if need_rg:
        r_ref[...] = jnp.dot(mix(0), wr_ref[...], preferred_element_type=F32)
        gg = jax.nn.sigmoid(jnp.dot(mix(5), g1_ref[...], preferred_element_type=F32))
        g_ref[...] = jnp.dot(gg.astype(BF16), g2_ref[...], preferred_element_type=F32)
    hw = jnp.tanh(jnp.dot(mix(1), w1_ref[...], preferred_element_type=F32))
    ha = jnp.dot(mix(4), a1_ref[...], preferred_element_type=F32)
    for c in range(d // LANES):
        sl = slice(c * LANES, (c + 1) * LANES)
        kc = k[:, sl] * kk_ref[:, sl]
        nrm = _segsum64(kc * kc, ones)
        kc = kc * lax.rsqrt(nrm + L2_EPS)
        nkk_ref[:, sl] = -kc
    kk = -nkk_ref[...]
    for dd, (dec_ref, kd_ref, bb_ref) in enumerate(((dec0_ref, kd0_ref, bb0_ref), (dec1_ref, kd1_ref, bb1_ref))):
        sel = (lane // (LANES // 2)) == dd
        z = w0_ref[dd:dd + 1, :] + jnp.dot(jnp.where(sel, hw, 0.0).astype(BF16), w2_ref[...],
                                            preferred_element_type=F32)
        nz = -z
        softplus = jnp.maximum(nz, 0.0) + jnp.log(1.0 + jnp.exp(-jnp.abs(nz)))
        logw = -softplus - 0.5
        dec_ref[...] = -jnp.exp(logw)
        a = jax.nn.sigmoid(a0_ref[dd:dd + 1, :] + jnp.dot(jnp.where(sel, ha, 0.0).astype(BF16), a2_ref[...],
                                                          preferred_element_type=F32))
        kd_ref[...] = k * (1.0 + (a - 1.0) * ka_ref[...])
        bb_ref[...] = kk * a


def rwkv_proj(xn, wts, latent, need_rg):
    b, t, d = xn.shape
    tr = min(PROJ_TILE, t)
    nt = t // tr
    halo = GRID_W
    rh = tr // halo
    nh = t // halo
    row = lambda bi, si: (bi, si, 0)
    n_out = 10 if need_rg else 8
    w_specs = [_full(w.shape) for w in wts]
    outs = pl.pallas_call(
        functools.partial(_rwkv_proj_kernel, latent=latent, need_rg=need_rg),
        grid=(b, nt),
        in_specs=[pl.BlockSpec((None, tr, d), row),
                  pl.BlockSpec((None, halo, d), lambda bi, si: (bi, jnp.maximum(si * rh - 1, 0), 0)),
                  pl.BlockSpec((None, halo, d), lambda bi, si: (bi, jnp.minimum((si + 1) * rh, nh - 1), 0))]
                 + w_specs,
        out_specs=[pl.BlockSpec((None, tr, d), row)] * n_out,
        out_shape=[jax.ShapeDtypeStruct((b, t, d), F32)] * n_out,
        scratch_shapes=[pltpu.VMEM((tr + 2 * halo, d), F32), pltpu.VMEM((tr, d), F32)],
        compiler_params=_cp(("parallel", "parallel")),
        name="rwkv_proj",
    )(xn, xn, xn, *wts)
    return outs


def _dot_nt(a, b):
    return lax.dot_general(a, b, (((1,), (1,)), ((), ())), preferred_element_type=F32)


def _wkv_pair_chunks(tiles, states, reverse, need_out):
    npair = len(states)
    c = tiles[0][0].shape[0]
    n2 = 2 * c
    each = lambda f, *ls: [f(*xs) for xs in zip(*ls)]
    dot = lambda a, b: jnp.dot(a, b, preferred_element_type=F32)
    bf = lambda z: z.astype(BF16)
    row = lax.broadcasted_iota(jnp.int32, (c, LANES), 0)
    head0 = lax.broadcasted_iota(jnp.int32, (c, LANES), 1) < RWKV_HEAD
    tr = lax.broadcasted_iota(jnp.int32, (n2, n2), 0)
    tc = lax.broadcasted_iota(jnp.int32, (n2, n2), 1)
    strict = (tr % c < tc % c) if reverse else (tr % c > tc % c)
    ident = jnp.where(tr == tc, 1.0, 0.0)

    def stack(z):
        return bf(jnp.concatenate([jnp.where(head0, z, 0.0), jnp.where(head0, 0.0, z)], axis=0))

    def cumsum(lw):
        cum = lw
        sh = 1
        while sh < c:
            if reverse:
                cum = cum + jnp.where(row < c - sh, pltpu.roll(cum, c - sh, 0), 0.0)
            else:
                cum = cum + jnp.where(row >= sh, pltpu.roll(cum, sh, 0), 0.0)
            sh *= 2
        return cum

    al, v, lw, kd, bb = [[tl[i] for tl in tiles] for i in range(5)]
    cum = each(cumsum, lw)
    total = [cm[0:1, :] if reverse else cm[c - 1:c, :] for cm in cum]
    pinv = [jnp.exp(-cm) for cm in cum]
    a_st = each(lambda a, cm, l: stack(a * jnp.exp(cm - l)), al, cum, lw)
    b_st = each(lambda b, p: stack(b * p), bb, pinv)
    k_st = each(lambda k, p: stack(k * p), kd, pinv)
    v_st = each(stack, v)
    if need_out:
        rbar = each(lambda r, cm: r * jnp.exp(cm), [tl[5] for tl in tiles], cum)
        x_st = each(lambda a, r: jnp.concatenate([a, stack(r)], axis=0), a_st, rbar)
    else:
        x_st = a_st
    gb = each(_dot_nt, x_st, b_st)
    gk = each(_dot_nt, x_st, k_st)
    mab = [jnp.where(strict, g[0:n2], 0.0) for g in gb]
    x = mab
    t = [ident + m for m in mab]
    power = 2
    while power < c:
        x = each(lambda z: dot(bf(z), bf(z)), x)
        t = each(lambda tt, z: tt + dot(bf(tt), bf(z)), t, x)
        power *= 2
    tb = each(bf, t)
    wm_st = each(dot, tb, a_st)
    mv = each(lambda g, vs: dot(bf(jnp.where(strict, g[0:n2], 0.0)), vs), gk, v_st)
    ut_st = each(lambda tt, m: dot(tt, bf(m)), tb, mv)
    wm = [w[0:c] + w[c:n2] for w in wm_st]
    ut = [u[0:c] + u[c:n2] for u in ut_st]
    sb = each(bf, states)
    if need_out:
        incl = jnp.logical_or(strict, tr % c == tc % c)
        arb = [bf(jnp.where(incl, g[n2:2 * n2], 0.0)) for g in gb]
        ark = [bf(jnp.where(incl, g[n2:2 * n2], 0.0)) for g in gk]
        qe_st = each(lambda a, w: dot(a, bf(w)), arb, wm_st)
        yi_st = each(lambda ak, vs, ab, u: dot(ak, vs) + dot(ab, bf(u)), ark, v_st, arb, ut_st)
        qe = each(lambda r, q: r + q[0:c] + q[c:n2], rbar, qe_st)
        uy = each(lambda w, q, s: _dot_nt(bf(jnp.concatenate([w, q], axis=0)), s), wm, qe, sb)
        u = each(lambda z, ut_: z[0:c] + ut_, uy, ut)
        ys = each(lambda z, yi: z[c:n2] + yi[0:c] + yi[c:n2], uy, yi_st)
    else:
        u = each(lambda w, s, ut_: _dot_nt(bf(w), s) + ut_, wm, sb, ut)
        ys = [None] * npair
    same_head = (lax.broadcasted_iota(jnp.int32, (LANES, LANES), 0) // RWKV_HEAD
                 == lax.broadcasted_iota(jnp.int32, (LANES, LANES), 1) // RWKV_HEAD)

    def new_state(s, vv, uu, k, b, cm, tot):
        tail = jnp.exp(tot - cm)
        vu = bf(jnp.concatenate([vv, uu], axis=0))
        kb = bf(jnp.concatenate([k * tail, b * tail], axis=0))
        ds = lax.dot_general(vu, kb, (((0,), (0,)), ((), ())), preferred_element_type=F32)
        return s * jnp.exp(tot) + jnp.where(same_head, ds, 0.0)

    return ys, each(new_state, states, v, u, kd, bb, cum, total)


def _scan_kernel(*refs, need_out, has_init):
    it = iter(refs)
    dirs = []
    for _ in range(2):
        dirs.append([next(it) for _ in range(6 if need_out else 5)])
    s0_ref = next(it) if has_init else None
    y_refs = [next(it), next(it)] if need_out else None
    sfin_ref = next(it)
    s_ref = next(it)
    ti = pl.program_id(1)
    nb, _, d = dirs[0][0].shape
    nchunk = d // LANES

    @pl.when(ti == 0)
    def _():
        if has_init:
            s_ref[...] = s0_ref[...]
        else:
            s_ref[...] = jnp.zeros_like(s_ref)

    for dd in range(2):
        refs_d = dirs[dd]

        def per_batch(bidx, carry, dd=dd, refs_d=refs_d):
            sls = [slice(c * LANES, (c + 1) * LANES) for c in range(nchunk)]
            tiles = [[ref[bidx, :, sl] for ref in refs_d] for sl in sls]
            states = [s_ref[dd, bidx, c] for c in range(nchunk)]
            ys, new_states = _wkv_pair_chunks(tiles, states, reverse=(dd == 1), need_out=need_out)
            for c in range(nchunk):
                s_ref[dd, bidx, c] = new_states[c]
                if need_out:
                    y_refs[dd][bidx, :, sls[c]] = ys[c]
            return carry

        lax.fori_loop(0, nb, per_batch, 0)

    @pl.when(ti == pl.num_programs(1) - 1)
    def _():
        sfin_ref[...] = s_ref[...]


def wkv_scan(nkk, v, lw, kd, bb, r, s0):
    b, t, d = v.shape
    tb = min(SCAN_CHUNK, t)
    nt = t // tb
    bg = math.gcd(b, SCAN_BATCH_GROUP)
    need_out = r is not None
    has_init = s0 is not None
    fwd = lambda g, i: (g, i, 0)
    bwd = lambda g, i: (g, nt - 1 - i, 0)
    args, in_specs = [], []
    for dd, imap in enumerate((fwd, bwd)):
        arrs = [nkk, v, lw[dd], kd[dd], bb[dd]] + ([r] if need_out else [])
        args += arrs
        in_specs += [pl.BlockSpec((bg, tb, d), imap)] * len(arrs)
    st_shape = (2, b, d // LANES, LANES, LANES)
    st_spec = pl.BlockSpec((2, bg, d // LANES, LANES, LANES), lambda g, i: (0, g, 0, 0, 0))
    if has_init:
        args.append(s0)
        in_specs.append(st_spec)
    out_specs, out_shape = [], []
    if need_out:
        out_specs += [pl.BlockSpec((bg, tb, d), fwd), pl.BlockSpec((bg, tb, d), bwd)]
        out_shape += [jax.ShapeDtypeStruct((b, t, d), F32)] * 2
    out_specs.append(st_spec)
    out_shape.append(jax.ShapeDtypeStruct(st_shape, F32))
    outs = pl.pallas_call(
        functools.partial(_scan_kernel, need_out=need_out, has_init=has_init),
        grid=(b // bg, nt),
        in_specs=in_specs,
        out_specs=out_specs,
        out_shape=out_shape,
        scratch_shapes=[pltpu.VMEM((2, bg, d // LANES, LANES, LANES), F32)],
        compiler_params=_cp(("arbitrary", "arbitrary")),
        name="wkv_scan",
    )(*args)
    if need_out:
        return outs[0], outs[1], outs[2]
    return None, None, outs[0]


def _readout_kernel(yf_ref, yb_ref, r_ref, v_ref, g_ref, kd0_ref, kd1_ref, rk_ref, gw_ref, gbias_ref,
                    wo_ref, h_ref, mods_ref, o_ref, z_ref):
    ones = _pair_ones()
    d = h_ref.shape[1]
    inv = 1.0 / RWKV_HEAD
    for c in range(d // LANES):
        sl = slice(c * LANES, (c + 1) * LANES)
        y = yf_ref[:, sl] + yb_ref[:, sl]
        mean = _segsum64(y, ones) * inv
        yc = y - mean
        var = _segsum64(yc * yc, ones) * inv
        yn = yc * lax.rsqrt(var + GN_EPS) * gw_ref[:, sl] + gbias_ref[:, sl]
        rkk = r_ref[:, sl] * (kd0_ref[:, sl] + kd1_ref[:, sl]) * rk_ref[:, sl]
        bonus = _segsum64(rkk, ones) * v_ref[:, sl]
        z_ref[:, sl] = ((yn + bonus) * g_ref[:, sl]).astype(BF16)
    out = jnp.dot(z_ref[...], wo_ref[...], preferred_element_type=F32)
    o_ref[...] = h_ref[...] + mods_ref[2:3, :] * out


def rwkv_readout(yf, yb, r, v, g, kd0, kd1, r_k, gn_w, gn_b, w_o, h, mods):
    b, t, d = h.shape
    tr = min(ROW_TILE, t)
    row = lambda bi, si: (bi, si, 0)
    rs = pl.BlockSpec((None, tr, d), row)
    return pl.pallas_call(
        _readout_kernel,
        grid=(b, t // tr),
        in_specs=[rs] * 7 + [_full((1, d))] * 3 + [_full((d, d)), rs,
                                                    pl.BlockSpec((None, 6, d), lambda bi, si: (bi, 0, 0))],
        out_specs=rs,
        out_shape=jax.ShapeDtypeStruct((b, t, d), F32),
        scratch_shapes=[pltpu.VMEM((tr, d), BF16)],
        compiler_params=_cp(("parallel", "parallel")),
        name="rwkv_readout",
    )(yf, yb, r, v, g, kd0, kd1, r_k.reshape(1, d), gn_w.reshape(1, d), gn_b.reshape(1, d), w_o, h, mods)


def _rope_tables(seq_len):
    t = jnp.arange(seq_len, dtype=jnp.int32)
    row = (t // GRID_W).astype(F32)
    col = (t % GRID_W).astype(F32)
    axis_dim = HEAD_DIM // 2
    inv_freq = ROPE_THETA ** (-jnp.arange(0, axis_dim, 2, dtype=F32) / axis_dim)
    ang = jnp.concatenate([row[:, None] * inv_freq, col[:, None] * inv_freq], axis=-1)
    cos = jnp.repeat(jnp.cos(ang), 2, axis=-1)
    sin = jnp.repeat(jnp.sin(ang), 2, axis=-1) * jnp.tile(jnp.array([-1.0, 1.0], F32), axis_dim)
    return jnp.tile(cos, (1, LANES // HEAD_DIM)), jnp.tile(sin, (1, LANES // HEAD_DIM))


def kernel(x, c, ctx, c_ctx, ada_w, ada_b, norm1_g, norm2_g, ev_w_in, ev_conv_w, ev_q_gain, ev_k_gain, ev_w_out,
           od_mu, od_w_r, od_w_k, od_w_v, od_w_o, od_g1, od_g2, od_k_k, od_k_a, od_r_k, od_w0, od_w1, od_w2,
           od_a0, od_a1, od_a2, od_gn_w, od_gn_b, peer_wq, peer_keys, peer_u, peer_v):
    b, s, d = x.shape
    lc = ctx.shape[1]
    depth = ada_w.shape[0]
    bf = lambda a: a.astype(BF16)

    pad_rows = (-(b + 1)) % 8
    cc = jnp.concatenate([c, c_ctx[None, :], jnp.zeros((pad_rows, d), F32)], axis=0)
    mods_all = ada_mods(cc, ada_w, ada_b)

    cos_l, sin_l = _rope_tables(s)
    cos_c = jnp.ones((lc, LANES), F32)
    sin_c = jnp.zeros((lc, LANES), F32)
    rep = LANES // HEAD_DIM

    hx, hc = x, ctx
    for i in range(depth):
        last = i == depth - 1
        j = i // 2
        mods_l = mods_all[i, :b].reshape(b, 6, d)
        mods_c = jnp.broadcast_to(mods_all[i, b].reshape(1, 6, d), (b, 6, d))
        if i % 2 == 0:
            w_in = bf(ev_w_in[j])
            w_out = bf(ev_w_out[j])
            qg = jnp.tile(ev_q_gain[j], rep).reshape(1, LANES)
            kg = jnp.tile(ev_k_gain[j], rep).reshape(1, LANES)
            u_l, gb_l, q_l, k_l, v_l = in_proj(hx, mods_l, norm1_g[i], w_in, qg, kg, cos_l, sin_l)
            u_c, gb_c, q_c, k_c, v_c = in_proj(hc, mods_c, norm1_g[i], w_in, qg, kg, cos_c, sin_c)
            att_l = attention(q_l, [(k_c, v_c), (k_l, v_l)])
            hx = conv_out(u_l, gb_l, att_l, ev_conv_w[j], w_out, hx, mods_l)
            if not last:
                att_c = attention(q_c, [(k_c, v_c)])
                hc = conv_out(u_c, gb_c, att_c, ev_conv_w[j], w_out, hc, mods_c)
        else:
            wts = [od_mu[j], bf(od_w_r[j]), bf(od_w_k[j]), bf(od_w_v[j]), bf(od_g1[j]), bf(od_g2[j]),
                   bf(jnp.concatenate([od_w1[j, 0], od_w1[j, 1]], axis=1)),
                   bf(jnp.concatenate([od_w2[j, 0], od_w2[j, 1]], axis=0)),
                   bf(jnp.concatenate([od_a1[j, 0], od_a1[j, 1]], axis=1)),
                   bf(jnp.concatenate([od_a2[j, 0], od_a2[j, 1]], axis=0)),
                   od_w0[j], od_a0[j], od_k_k[j].reshape(1, d), od_k_a[j].reshape(1, d)]
            xn_c = norm_mod(hc, mods_c, norm1_g[i])
            xn_l = norm_mod(hx, mods_l, norm1_g[i])
            pc = rwkv_proj(xn_c, wts, latent=False, need_rg=not last)
            pl_ = rwkv_proj(xn_l, wts, latent=True, need_rg=True)
            if last:
                v_c, nkk_c, d0c, d1c, kd0c, kd1c, bb0c, bb1c = pc
                r_c = g_c = None
            else:
                r_c, g_c, v_c, nkk_c, d0c, d1c, kd0c, kd1c, bb0c, bb1c = pc
            r_l, g_l, v_l, nkk_l, d0l, d1l, kd0l, kd1l, bb0l, bb1l = pl_
            yf_c, yb_c, s_ctx = wkv_scan(nkk_c, v_c, (d0c, d1c), (kd0c, kd1c), (bb0c, bb1c), r_c, None)
            yf_l, yb_l, _ = wkv_scan(nkk_l, v_l, (d0l, d1l), (kd0l, kd1l), (bb0l, bb1l), r_l, s_ctx)
            r_k = od_r_k[j].reshape(d)
            hx = rwkv_readout(yf_l, yb_l, r_l, v_l, g_l, kd0l, kd1l, r_k, od_gn_w[j], od_gn_b[j],
                              bf(od_w_o[j]), hx, mods_l)
            if not last:
                hc = rwkv_readout(yf_c, yb_c, r_c, v_c, g_c, kd0c, kd1c, r_k, od_gn_w[j], od_gn_b[j],
                                  bf(od_w_o[j]), hc, mods_c)
        wq = bf(peer_wq[i])
        keys = bf(peer_keys[i]).reshape(PEER_HEADS * 2, PEER_NKEYS, PEER_HALF)
        u_tab = bf(peer_u[i])
        v_tab = bf(peer_v[i])
        hx = peer_ffn_residual(hx, mods_l, norm2_g[i], wq, keys, u_tab, v_tab)
        if not last:
            hc = peer_ffn_residual(hc, mods_c, norm2_g[i], wq, keys, u_tab, v_tab)
    return hx
```

```python
import functools
import math

import jax
import jax.numpy as jnp
from jax import lax
from jax.experimental import pallas as pl
from jax.experimental.pallas import tpu as pltpu

F32 = jnp.float32
BF16 = jnp.bfloat16

GRID_W = 64
RMS_EPS = 1e-6
HEAD_DIM = 64
CONV_CH = 512
ATT_Q_HEADS = 8
ATT_KV_HEADS = 2
ATT_GROUP = ATT_Q_HEADS // ATT_KV_HEADS
ATT_SCALE = HEAD_DIM ** -0.5
ROPE_THETA = 10000.0
RWKV_HEAD = 64
GN_EPS = 64e-5
L2_EPS = 1e-12
PEER_HEADS = 8
PEER_NKEYS = 128
PEER_TOPK = 16
PEER_HALF = 128

LANES = 128
SUBLANES = 8
ROW_TILE = 256
ATT_TILE = 256
PROJ_TILE = 128
PEER_TOK_TILE = 256
PEER_EXP_CHUNK = 2048
PEER_SCATTER_UNROLL = 4
SCAN_CHUNK = 64
SCAN_BATCH_GROUP = 4
VMEM_LIMIT = 56 * 1024 * 1024


def _cp(sem, vmem=VMEM_LIMIT):
    return pltpu.CompilerParams(dimension_semantics=sem, vmem_limit_bytes=vmem)


def _full(shape):
    n = len(shape)
    return pl.BlockSpec(shape, lambda *_: (0,) * n)


def _pair_ones():
    r = lax.broadcasted_iota(jnp.int32, (LANES, LANES), 0) // HEAD_DIM
    c = lax.broadcasted_iota(jnp.int32, (LANES, LANES), 1) // HEAD_DIM
    return (r == c).astype(BF16)


def _segsum64(x, ones):
    hi = x.astype(BF16)
    lo = (x - hi.astype(F32)).astype(BF16)
    return (jnp.dot(hi, ones, preferred_element_type=F32)
            + jnp.dot(lo, ones, preferred_element_type=F32))


def _norm_mod(x, g, shift, scale):
    ms = jnp.mean(x * x, axis=-1, keepdims=True)
    return (x * lax.rsqrt(ms + RMS_EPS) * g) * (1.0 + scale) + shift


def _ada_kernel(c_ref, w_ref, b_ref, o_ref):
    c = c_ref[...]
    s = c * jax.nn.sigmoid(c)
    o_ref[...] = jnp.dot(s.astype(BF16), w_ref[...].astype(BF16), preferred_element_type=F32) + b_ref[...]


def ada_mods(cc, ada_w, ada_b):
    depth, d, d6 = ada_w.shape
    r = cc.shape[0]
    nblk = d6 // d
    return pl.pallas_call(
        _ada_kernel,
        grid=(depth, nblk),
        in_specs=[pl.BlockSpec((r, d), lambda i, j: (0, 0)),
                  pl.BlockSpec((None, d, d), lambda i, j: (i, 0, j)),
                  pl.BlockSpec((None, 1, d), lambda i, j: (i, 0, j))],
        out_specs=pl.BlockSpec((None, r, d), lambda i, j: (i, 0, j)),
        out_shape=jax.ShapeDtypeStruct((depth, r, d6), F32),
        compiler_params=_cp(("parallel", "parallel")),
        name="ada_mods",
    )(cc, ada_w, ada_b.reshape(depth, 1, d6))


def _inproj_kernel(h_ref, mods_ref, g_ref, w_ref, qg_ref, kg_ref, cos_ref, sin_ref,
                   u_ref, gb_ref, q_ref, k_ref, v_ref):
    ones = _pair_ones()
    x = h_ref[...]
    xn = _norm_mod(x, g_ref[...], mods_ref[0:1, :], mods_ref[1:2, :])
    y = jnp.dot(xn.astype(BF16), w_ref[...], preferred_element_type=F32)
    c = CONV_CH
    u_ref[...] = y[:, 2 * c:3 * c] * y[:, 0:c]
    gb_ref[...] = y[:, c:2 * c]
    cos = cos_ref[...]
    sin = sin_ref[...]
    lane = lax.broadcasted_iota(jnp.int32, (x.shape[0], LANES), 1)
    even = (lane % 2) == 0

    def qk_norm_rope(z, gain):
        ss = _segsum64(z * z, ones) * (1.0 / HEAD_DIM)
        zn = z * lax.rsqrt(ss + RMS_EPS) * gain
        partner = jnp.where(even, pltpu.roll(zn, LANES - 1, 1), pltpu.roll(zn, 1, 1))
        return zn * cos + partner * sin

    q0 = 3 * c
    for j in range(ATT_Q_HEADS * HEAD_DIM // LANES):
        z = y[:, q0 + j * LANES:q0 + (j + 1) * LANES]
        q_ref[:, j * LANES:(j + 1) * LANES] = qk_norm_rope(z, qg_ref[...]) * ATT_SCALE
    k0 = q0 + ATT_Q_HEADS * HEAD_DIM
    k_ref[...] = qk_norm_rope(y[:, k0:k0 + LANES], kg_ref[...]).astype(BF16)
    v_ref[...] = y[:, k0 + LANES:k0 + 2 * LANES].astype(BF16)


def in_proj(h, mods, g, w_in, q_gain, k_gain, cos_t, sin_t):
    b, t, d = h.shape
    tr = min(ROW_TILE, t)
    nt = t // tr
    proj = w_in.shape[1]
    row = lambda bi, si: (bi, si, 0)
    outs = pl.pallas_call(
        _inproj_kernel,
        grid=(b, nt),
        in_specs=[pl.BlockSpec((None, tr, d), row),
                  pl.BlockSpec((None, 6, d), lambda bi, si: (bi, 0, 0)),
                  _full((1, d)), _full((d, proj)), _full((1, LANES)), _full((1, LANES)),
                  pl.BlockSpec((tr, LANES), lambda bi, si: (si, 0)),
                  pl.BlockSpec((tr, LANES), lambda bi, si: (si, 0))],
        out_specs=[pl.BlockSpec((None, tr, CONV_CH), row),
                   pl.BlockSpec((None, tr, CONV_CH), row),
                   pl.BlockSpec((None, tr, CONV_CH), row),
                   pl.BlockSpec((None, tr, LANES), row),
                   pl.BlockSpec((None, tr, LANES), row)],
        out_shape=[jax.ShapeDtypeStruct((b, t, CONV_CH), F32),
                   jax.ShapeDtypeStruct((b, t, CONV_CH), F32),
                   jax.ShapeDtypeStruct((b, t, CONV_CH), F32),
                   jax.ShapeDtypeStruct((b, t, LANES), BF16),
                   jax.ShapeDtypeStruct((b, t, LANES), BF16)],
        compiler_params=_cp(("parallel", "parallel")),
        name="in_proj",
    )(h, mods, g.reshape(1, d), w_in, q_gain, k_gain, cos_t, sin_t)
    return outs


def _attn_kernel(*refs, n_sets):
    q_ref = refs[0]
    kv = refs[1:1 + 2 * n_sets]
    o_ref = refs[1 + 2 * n_sets]
    tq = q_ref.shape[0]
    lane = lax.broadcasted_iota(jnp.int32, (tq, LANES), 1)
    half = lane // HEAD_DIM
    nt = (((1,), (1,)), ((), ()))
    for cchunk in range(ATT_Q_HEADS * HEAD_DIM // LANES):
        qc = q_ref[:, cchunk * LANES:(cchunk + 1) * LANES]
        out_c = jnp.zeros((tq, LANES), F32)
        for hh in range(2):
            h = 2 * cchunk + hh
            j = h // ATT_GROUP
            qa = qc if hh == j else pltpu.roll(qc, HEAD_DIM, 1)
            qm = jnp.where(half == j, qa, 0.0).astype(BF16)
            ss = [lax.dot_general(qm, kv[2 * i][...], nt, preferred_element_type=F32)
                  for i in range(n_sets)]
            m = ss[0].max(axis=-1, keepdims=True)
            for s in ss[1:]:
                m = jnp.maximum(m, s.max(axis=-1, keepdims=True))
            l = jnp.zeros((tq, 1), F32)
            o = jnp.zeros((tq, LANES), F32)
            for i in range(n_sets):
                p = jnp.exp(ss[i] - m)
                l = l + p.sum(axis=-1, keepdims=True)
                o = o + jnp.dot(p.astype(BF16), kv[2 * i + 1][...], preferred_element_type=F32)
            o = o / l
            oa = o if hh == j else pltpu.roll(o, HEAD_DIM, 1)
            out_c = jnp.where(half == hh, oa, out_c)
        o_ref[:, cchunk * LANES:(cchunk + 1) * LANES] = out_c


def attention(q, kv_sets):
    b, t, qd = q.shape
    tq = min(ATT_TILE, t)
    in_specs = [pl.BlockSpec((None, tq, qd), lambda bi, si: (bi, si, 0))]
    args = [q]
    for k, v in kv_sets:
        ln = k.shape[1]
        in_specs += [pl.BlockSpec((None, ln, LANES), lambda bi, si: (bi, 0, 0))] * 2
        args += [k, v]
    return pl.pallas_call(
        functools.partial(_attn_kernel, n_sets=len(kv_sets)),
        grid=(b, t // tq),
        in_specs=in_specs,
        out_specs=pl.BlockSpec((None, tq, qd), lambda bi, si: (bi, si, 0)),
        out_shape=jax.ShapeDtypeStruct((b, t, qd), F32),
        compiler_params=_cp(("parallel", "parallel")),
        name="attention",
    )(*args)


def _conv_out_kernel(u_ref, up_ref, un_ref, gb_ref, att_ref, cw_ref, wo_ref, h_ref, mods_ref, o_ref):
    si = pl.program_id(1)
    last = pl.num_programs(1) - 1
    u = u_ref[...]
    tr = u.shape[0]
    row = lax.broadcasted_iota(jnp.int32, u.shape, 0)
    prev_row = jnp.where(si == 0, 0.0, up_ref[7:8, :])
    next_row = jnp.where(si == last, 0.0, un_ref[0:1, :])
    u_m1 = jnp.where(row == 0, prev_row, pltpu.roll(u, 1, 0))
    u_p1 = jnp.where(row == tr - 1, next_row, pltpu.roll(u, tr - 1, 0))
    conv = u_m1 * cw_ref[0:1, :] + u * cw_ref[1:2, :] + u_p1 * cw_ref[2:3, :]
    conv = gb_ref[...] * conv
    c = CONV_CH
    y = (jnp.dot(conv.astype(BF16), wo_ref[0:c, :], preferred_element_type=F32)
         + jnp.dot(att_ref[...].astype(BF16), wo_ref[c:2 * c, :], preferred_element_type=F32))
    o_ref[...] = h_ref[...] + mods_ref[2:3, :] * y


def conv_out(u, gb, att, conv_w, w_out, h, mods):
    b, t, d = h.shape
    tr = min(ROW_TILE, t)
    nt = t // tr
    r8 = tr // 8
    n8 = t // 8
    row = lambda bi, si: (bi, si, 0)
    return pl.pallas_call(
        _conv_out_kernel,
        grid=(b, nt),
        in_specs=[pl.BlockSpec((None, tr, CONV_CH), row),
                  pl.BlockSpec((None, 8, CONV_CH), lambda bi, si: (bi, jnp.maximum(si * r8 - 1, 0), 0)),
                  pl.BlockSpec((None, 8, CONV_CH), lambda bi, si: (bi, jnp.minimum((si + 1) * r8, n8 - 1), 0)),
                  pl.BlockSpec((None, tr, CONV_CH), row),
                  pl.BlockSpec((None, tr, CONV_CH), row),
                  _full((3, CONV_CH)), _full((d, d)),
                  pl.BlockSpec((None, tr, d), row),
                  pl.BlockSpec((None, 6, d), lambda bi, si: (bi, 0, 0))],
        out_specs=pl.BlockSpec((None, tr, d), row),
        out_shape=jax.ShapeDtypeStruct((b, t, d), F32),
        compiler_params=_cp(("parallel", "parallel")),
        name="conv_out",
    )(u, u, u, gb, att, conv_w, w_out, h, mods)


def _topk_rows(s, k, pos=None):
    n, t = s.shape
    if pos is None:
        pos = lax.broadcasted_iota(jnp.int32, (n, t), 0).astype(F32)
    krow = lax.broadcasted_iota(jnp.int32, (k, t), 0)
    vals = jnp.zeros((k, t), F32)
    idxs = jnp.zeros((k, t), F32)
    for it in range(k):
        m = jnp.max(s, axis=0, keepdims=True)
        idx = jnp.min(jnp.where(s == m, pos, jnp.inf), axis=0, keepdims=True)
        s = jnp.where(pos == idx, -jnp.inf, s)
        vals = jnp.where(krow == it, m, vals)
        idxs = jnp.where(krow == it, idx, idxs)
    return vals, idxs


PEER_NCAND = PEER_TOPK + (PEER_TOPK // 2 - 1) * SUBLANES + PEER_TOPK // 2


def _router_kernel(h_ref, mods_ref, g_ref, wq_ref, keys_ref, xn_ref, a_ref, b_ref, gate_ref,
                   cand_ref, res_ref):
    x = h_ref[...]
    t = x.shape[0]
    xn = _norm_mod(x, g_ref[...], mods_ref[3:4, :], mods_ref[4:5, :])
    xb = xn.astype(BF16)
    xn_ref[...] = xb
    q = jnp.dot(xb, wq_ref[...], preferred_element_type=F32).astype(BF16)
    nt = (((1,), (1,)), ((), ()))
    kk = PEER_TOPK
    j8 = lax.broadcasted_iota(jnp.int32, (SUBLANES, t), 0)
    j8f = j8.astype(F32)
    cand_pos = jnp.concatenate(
        [lax.broadcasted_iota(jnp.int32, (kk, t), 0).astype(F32)]
        + [j8f + float(i * kk) for i in range(1, kk // 2)]
        + [(j8f + float(kk // 2)) * float(kk)], axis=0)
    ncand = cand_pos.shape[0]
    for h in range(PEER_HEADS):
        tops = []
        for p in range(2):
            g = 2 * h + p
            st = lax.dot_general(keys_ref[g], q[:, g * PEER_HALF:(g + 1) * PEER_HALF], nt,
                                 preferred_element_type=F32)
            tops.append(_topk_rows(st, kk))
        (s1, i1), (s2, i2) = tops
        cand_ref[0:kk, :] = s1[0:1, :] + s2
        for i in range(1, kk // 2):
            r0 = kk + (i - 1) * SUBLANES
            cand_ref[r0:r0 + SUBLANES, :] = jnp.where(j8 < kk // (i + 1), s1[i:i + 1, :] + s2[0:SUBLANES, :],
                                                      -jnp.inf)
        cand_ref[ncand - kk // 2:ncand, :] = s1[kk // 2:kk, :] + s2[0:1, :]
        top_s, pos = _topk_rows(cand_ref[...], kk, cand_pos)
        p1 = jnp.floor(pos * (1.0 / kk))
        p2 = pos - p1 * kk
        a_idx = jnp.zeros((kk, t), F32)
        b_idx = jnp.zeros((kk, t), F32)
        for k1 in range(kk):
            a_idx = a_idx + jnp.where(p1 == k1, i1[k1:k1 + 1, :], 0.0)
            b_idx = b_idx + jnp.where(p2 == k1, i2[k1:k1 + 1, :], 0.0)
        e = jnp.exp(top_s - top_s[0:1, :])
        gate = e / jnp.sum(e, axis=0, keepdims=True)
        res_ref[0, h * kk:(h + 1) * kk, :] = a_idx
        res_ref[1, h * kk:(h + 1) * kk, :] = b_idx
        res_ref[2, h * kk:(h + 1) * kk, :] = gate
    a_ref[...] = res_ref[0].T.astype(jnp.int32)
    b_ref[...] = res_ref[1].T.astype(jnp.int32)
    gate_ref[...] = res_ref[2].T


def peer_router(h, mods, g, wq, keys):
    b, t, d = h.shape
    tr = min(ROW_TILE, t)
    row = lambda bi, si: (bi, si, 0)
    hk = PEER_HEADS * PEER_TOPK
    return pl.pallas_call(
        _router_kernel,
        grid=(b, t // tr),
        in_specs=[pl.BlockSpec((None, tr, d), row),
                  pl.BlockSpec((None, 6, d), lambda bi, si: (bi, 0, 0)),
                  _full((1, d)), _full(wq.shape), _full(keys.shape)],
        out_specs=[pl.BlockSpec((None, tr, d), row),
                   pl.BlockSpec((None, tr, hk), row),
                   pl.BlockSpec((None, tr, hk), row),
                   pl.BlockSpec((None, tr, hk), row)],
        out_shape=[jax.ShapeDtypeStruct((b, t, d), BF16),
                   jax.ShapeDtypeStruct((b, t, hk), jnp.int32),
                   jax.ShapeDtypeStruct((b, t, hk), jnp.int32),
                   jax.ShapeDtypeStruct((b, t, hk), F32)],
        scratch_shapes=[pltpu.VMEM((PEER_NCAND, tr), F32),
                        pltpu.VMEM((3, hk, tr), F32)],
        compiler_params=_cp(("parallel", "parallel")),
        name="peer_router",
    )(h, mods, g.reshape(1, d), wq, keys)


def _peer_dense_kernel(xn_ref, a_ref, b_ref, gate_ref, u_ref, v_ref, h_ref, mods_ref, o_ref,
                       g_ref, acc_ref, w_ref):
    j = pl.program_id(1)
    tt = xn_ref.shape[0]
    nk = PEER_NKEYS
    nt = (((1,), (1,)), ((), ()))

    @pl.when(j == 0)
    def _():
        acc_ref[...] = jnp.zeros_like(acc_ref)
        sub = lax.broadcasted_iota(jnp.int32, (nk, a_ref.shape[1]), 0)

        def per_groups(i0, carry):
            for gi in range(PEER_SCATTER_UNROLL):
                i = i0 * PEER_SCATTER_UNROLL + gi
                rows = pl.ds(pl.multiple_of(i * SUBLANES, SUBLANES), SUBLANES)
                ai8, bi8, gt8 = a_ref[rows, :], b_ref[rows, :], gate_ref[rows, :]
                for k in range(SUBLANES):
                    at = jnp.where(sub == ai8[k:k + 1, :], gt8[k:k + 1, :], 0.0).astype(BF16)
                    bt = (sub == bi8[k:k + 1, :]).astype(BF16)
                    g_ref[i, pl.ds(k, nk, stride=SUBLANES), :] = lax.dot_general(
                        at, bt, nt, preferred_element_type=F32)
            return carry

        lax.fori_loop(0, tt // (SUBLANES * PEER_SCATTER_UNROLL), per_groups, 0)

    ec = u_ref.shape[0]
    halves = [slice(p * (ec // 2), (p + 1) * (ec // 2)) for p in range(2)]
    xb = xn_ref[...]
    hmats = [lax.dot_general(xb, u_ref[sl, :], nt, preferred_element_type=F32) for sl in halves]
    nblk = ec // 2 // nk
    acc = acc_ref[...]
    for p, sl in enumerate(halves):
        for blk in range(nblk):
            hb = hmats[p][:, blk * nk:(blk + 1) * nk]
            act = 0.5 * hb * (1.0 + lax.erf(hb * (1.0 / math.sqrt(2.0))))
            a0 = (j * 2 + p) * nblk + blk
            gate = g_ref[:, pl.ds(pl.multiple_of(a0 * SUBLANES, SUBLANES), SUBLANES), :].reshape(tt, nk)
            w_ref[:, sl.start + blk * nk:sl.start + (blk + 1) * nk] = (gate * act).astype(BF16)
        acc = acc + jnp.dot(w_ref[:, sl], v_ref[sl, :], preferred_element_type=F32)
    acc_ref[...] = acc

    @pl.when(j == pl.num_programs(1) - 1)
    def _():
        o_ref[...] = h_ref[...] + mods_ref[5:6, :] * acc_ref[...]


def peer_dense(xn, a_idx, b_idx, gate, u_tab, v_tab, h, mods):
    b, t, d = h.shape
    n = b * t
    tt = min(PEER_TOK_TILE, t)
    tiles_per_seq = t // tt
    ne = u_tab.shape[0]
    ec = PEER_EXP_CHUNK
    hk = a_idx.shape[-1]
    tok = lambda i, j: (i, 0)
    out = pl.pallas_call(
        _peer_dense_kernel,
        grid=(n // tt, ne // ec),
        in_specs=[pl.BlockSpec((tt, d), tok),
                  pl.BlockSpec((tt, hk), tok), pl.BlockSpec((tt, hk), tok), pl.BlockSpec((tt, hk), tok),
                  pl.BlockSpec((ec, d), lambda i, j: (j, 0)),
                  pl.BlockSpec((ec, d), lambda i, j: (j, 0)),
                  pl.BlockSpec((tt, d), tok),
                  pl.BlockSpec((None, 6, d), lambda i, j: (i // tiles_per_seq, 0, 0))],
        out_specs=pl.BlockSpec((tt, d), tok),
        out_shape=jax.ShapeDtypeStruct((n, d), F32),
        scratch_shapes=[pltpu.VMEM((tt // SUBLANES, PEER_NKEYS * SUBLANES, PEER_NKEYS), F32),
                        pltpu.VMEM((tt, d), F32),
                        pltpu.VMEM((tt, ec), BF16)],
        compiler_params=_cp(("parallel", "arbitrary")),
        name="peer_dense",
    )(xn.reshape(n, d), a_idx.reshape(n, hk), b_idx.reshape(n, hk), gate.reshape(n, hk),
      u_tab, v_tab, h.reshape(n, d), mods)
    return out.reshape(b, t, d)


def peer_ffn_residual(h, mods, g, wq, keys, u_tab, v_tab):
    xn, a_idx, b_idx, gate = peer_router(h, mods, g, wq, keys)
    return peer_dense(xn, a_idx, b_idx, gate, u_tab, v_tab, h, mods)


def _normmod_kernel(h_ref, mods_ref, g_ref, o_ref):
    o_ref[...] = _norm_mod(h_ref[...], g_ref[...], mods_ref[0:1, :], mods_ref[1:2, :])


def norm_mod(h, mods, g):
    b, t, d = h.shape
    tr = min(ROW_TILE, t)
    row = lambda bi, si: (bi, si, 0)
    return pl.pallas_call(
        _normmod_kernel,
        grid=(b, t // tr),
        in_specs=[pl.BlockSpec((None, tr, d), row),
                  pl.BlockSpec((None, 6, d), lambda bi, si: (bi, 0, 0)),
                  _full((1, d))],
        out_specs=pl.BlockSpec((None, tr, d), row),
        out_shape=jax.ShapeDtypeStruct((b, t, d), F32),
        compiler_params=_cp(("parallel", "parallel")),
        name="norm_mod",
    )(h, mods, g.reshape(1, d))


def _rwkv_proj_kernel(x_ref, xp_ref, xnx_ref, mu_ref, wr_ref, wk_ref, wv_ref, g1_ref, g2_ref,
                      w1_ref, w2_ref, a1_ref, a2_ref, w0_ref, a0_ref, kk_ref, ka_ref,
                      *out_and_scratch, latent, need_rg):
    if need_rg:
        (r_ref, g_ref, v_ref, nkk_ref, dec0_ref, dec1_ref, kd0_ref, kd1_ref, bb0_ref, bb1_ref,
         ext_ref, sh_ref) = out_and_scratch
    else:
        (v_ref, nkk_ref, dec0_ref, dec1_ref, kd0_ref, kd1_ref, bb0_ref, bb1_ref,
         ext_ref, sh_ref) = out_and_scratch
    si = pl.program_id(1)
    last = pl.num_programs(1) - 1
    halo = xp_ref.shape[0]
    t, d = x_ref.shape
    x = x_ref[...]
    ext_ref[0:halo, :] = jnp.where(si == 0, 0.0, xp_ref[...])
    ext_ref[halo:halo + t, :] = x
    ext_ref[halo + t:halo + t + halo, :] = jnp.where(si == last, 0.0, xnx_ref[...])
    if latent:
        q = d // 4
        col = lax.broadcasted_iota(jnp.int32, (t, q), 0) % GRID_W
        sh_ref[:, 0:q] = jnp.where(col == 0, 0.0, ext_ref[halo - 1:halo - 1 + t, 0:q])
        sh_ref[:, q:2 * q] = jnp.where(col == GRID_W - 1, 0.0, ext_ref[halo + 1:halo + 1 + t, q:2 * q])
        sh_ref[:, 2 * q:3 * q] = ext_ref[halo - GRID_W:halo - GRID_W + t, 2 * q:3 * q]
        sh_ref[:, 3 * q:4 * q] = ext_ref[halo + GRID_W:halo + GRID_W + t, 3 * q:4 * q]
    else:
        hd = d // 2
        sh_ref[:, 0:hd] = ext_ref[halo - 1:halo - 1 + t, 0:hd]
        sh_ref[:, hd:d] = ext_ref[halo + 1:halo + 1 + t, hd:d]
    xx = sh_ref[...] - x

    def mix(m):
        return (x + xx * mu_ref[m:m + 1, :]).astype(BF16)

    ones = _pair_ones()
    lane = lax.broadcasted_iota(jnp.int32, (t, LANES), 1)
    k = jnp.dot(mix(2), wk_ref[...], preferred_element_type=F32)
    v_ref[...] = jnp.dot(mix(3), wv_ref[...], preferred_element_type=F32)
    if need_rg:
        r_ref[...] = jnp.dot(mix(0), wr_ref[...], preferred_element_type=F32)
        gg = jax.nn.sigmoid(jnp.dot(mix(5), g1_ref[...], preferred_element_type=F32))
        g_ref[...] = jnp.dot(gg.astype(BF16), g2_ref[...], preferred_element_type=F32)
    hw = jnp.tanh(jnp.dot(mix(1), w1_ref[...], preferred_element_type=F32))
    ha = jnp.dot(mix(4), a1_ref[...], preferred_element_type=F32)
    for c in range(d // LANES):
        sl = slice(c * LANES, (c + 1) * LANES)
        kc = k[:, sl] * kk_ref[:, sl]
        nrm = _segsum64(kc * kc, ones)
        kc = kc * lax.rsqrt(nrm + L2_EPS)
        nkk_ref[:, sl] = -kc
    kk = -nkk_ref[...]
    for dd, (dec_ref, kd_ref, bb_ref) in enumerate(((dec0_ref, kd0_ref, bb0_ref), (dec1_ref, kd1_ref, bb1_ref))):
        sel = (lane // (LANES // 2)) == dd
        z = w0_ref[dd:dd + 1, :] + jnp.dot(jnp.where(sel, hw, 0.0).astype(BF16), w2_ref[...],
                                            preferred_element_type=F32)
        nz = -z
        softplus = jnp.maximum(nz, 0.0) + jnp.log(1.0 + jnp.exp(-jnp.abs(nz)))
        logw = -softplus - 0.5
        dec_ref[...] = -jnp.exp(logw)
        a = jax.nn.sigmoid(a0_ref[dd:dd + 1, :] + jnp.dot(jnp.where(sel, ha, 0.0).astype(BF16), a2_ref[...],
                                                          preferred_element_type=F32))
        kd_ref[...] = k * (1.0 + (a - 1.0) * ka_ref[...])
        bb_ref[...] = kk * a


def rwkv_proj(xn, wts, latent, need_rg):
    b, t, d = xn.shape
    tr = min(PROJ_TILE, t)
    nt = t // tr
    halo = GRID_W
    rh = tr // halo
    nh = t // halo
    row = lambda bi, si: (bi, si, 0)
    n_out = 10 if need_rg else 8
    w_specs = [_full(w.shape) for w in wts]
    outs = pl.pallas_call(
        functools.partial(_rwkv_proj_kernel, latent=latent, need_rg=need_rg),
        grid=(b, nt),
        in_specs=[pl.BlockSpec((None, tr, d), row),
                  pl.BlockSpec((None, halo, d), lambda bi, si: (bi, jnp.maximum(si * rh - 1, 0), 0)),
                  pl.BlockSpec((None, halo, d), lambda bi, si: (bi, jnp.minimum((si + 1) * rh, nh - 1), 0))]
                 + w_specs,
        out_specs=[pl.BlockSpec((None, tr, d), row)] * n_out,
        out_shape=[jax.ShapeDtypeStruct((b, t, d), F32)] * n_out,
        scratch_shapes=[pltpu.VMEM((tr + 2 * halo, d), F32), pltpu.VMEM((tr, d), F32)],
        compiler_params=_cp(("parallel", "parallel")),
        name="rwkv_proj",
    )(xn, xn, xn, *wts)
    return outs


def _dot_nt(a, b):
    return lax.dot_general(a, b, (((1,), (1,)), ((), ())), preferred_element_type=F32)


def _wkv_pair_chunks(tiles, states, reverse, need_out):
    npair = len(states)
    c = tiles[0][0].shape[0]
    n2 = 2 * c
    each = lambda f, *ls: [f(*xs) for xs in zip(*ls)]
    dot = lambda a, b: jnp.dot(a, b, preferred_element_type=F32)
    bf = lambda z: z.astype(BF16)
    row = lax.broadcasted_iota(jnp.int32, (c, LANES), 0)
    head0 = lax.broadcasted_iota(jnp.int32, (c, LANES), 1) < RWKV_HEAD
    tr = lax.broadcasted_iota(jnp.int32, (n2, n2), 0)
    tc = lax.broadcasted_iota(jnp.int32, (n2, n2), 1)
    strict = (tr % c < tc % c) if reverse else (tr % c > tc % c)
    ident = jnp.where(tr == tc, 1.0, 0.0)

    def stack(z):
        return bf(jnp.concatenate([jnp.where(head0, z, 0.0), jnp.where(head0, 0.0, z)], axis=0))

    def cumsum(lw):
        cum = lw
        sh = 1
        while sh < c:
            if reverse:
                cum = cum + jnp.where(row < c - sh, pltpu.roll(cum, c - sh, 0), 0.0)
            else:
                cum = cum + jnp.where(row >= sh, pltpu.roll(cum, sh, 0), 0.0)
            sh *= 2
        return cum

    al, v, lw, kd, bb = [[tl[i] for tl in tiles] for i in range(5)]
    cum = each(cumsum, lw)
    total = [cm[0:1, :] if reverse else cm[c - 1:c, :] for cm in cum]
    pinv = [jnp.exp(-cm) for cm in cum]
    a_st = each(lambda a, cm, l: stack(a * jnp.exp(cm - l)), al, cum, lw)
    b_st = each(lambda b, p: stack(b * p), bb, pinv)
    k_st = each(lambda k, p: stack(k * p), kd, pinv)
    v_st = each(stack, v)
    if need_out:
        rbar = each(lambda r, cm: r * jnp.exp(cm), [tl[5] for tl in tiles], cum)
        x_st = each(lambda a, r: jnp.concatenate([a, stack(r)], axis=0), a_st, rbar)
    else:
        x_st = a_st
    gb = each(_dot_nt, x_st, b_st)
    gk = each(_dot_nt, x_st, k_st)
    mab = [jnp.where(strict, g[0:n2], 0.0) for g in gb]
    x = mab
    t = [ident + m for m in mab]
    power = 2
    while power < c:
        x = each(lambda z: dot(bf(z), bf(z)), x)
        t = each(lambda tt, z: tt + dot(bf(tt), bf(z)), t, x)
        power *= 2
    tb = each(bf, t)
    wm_st = each(dot, tb, a_st)
    mv = each(lambda g, vs: dot(bf(jnp.where(strict, g[0:n2], 0.0)), vs), gk, v_st)
    ut_st = each(lambda tt, m: dot(tt, bf(m)), tb, mv)
    wm = [w[0:c] + w[c:n2] for w in wm_st]
    ut = [u[0:c] + u[c:n2] for u in ut_st]
    sb = each(bf, states)
    if need_out:
        incl = jnp.logical_or(strict, tr % c == tc % c)
        arb = [bf(jnp.where(incl, g[n2:2 * n2], 0.0)) for g in gb]
        ark = [bf(jnp.where(incl, g[n2:2 * n2], 0.0)) for g in gk]
        qe_st = each(lambda a, w: dot(a, bf(w)), arb, wm_st)
        yi_st = each(lambda ak, vs, ab, u: dot(ak, vs) + dot(ab, bf(u)), ark, v_st, arb, ut_st)
        qe = each(lambda r, q: r + q[0:c] + q[c:n2], rbar, qe_st)
        uy = each(lambda w, q, s: _dot_nt(bf(jnp.concatenate([w, q], axis=0)), s), wm, qe, sb)
        u = each(lambda z, ut_: z[0:c] + ut_, uy, ut)
        ys = each(lambda z, yi: z[c:n2] + yi[0:c] + yi[c:n2], uy, yi_st)
    else:
        u = each(lambda w, s, ut_: _dot_nt(bf(w), s) + ut_, wm, sb, ut)
        ys = [None] * npair
    same_head = (lax.broadcasted_iota(jnp.int32, (LANES, LANES), 0) // RWKV_HEAD
                 == lax.broadcasted_iota(jnp.int32, (LANES, LANES), 1) // RWKV_HEAD)

    def new_state(s, vv, uu, k, b, cm, tot):
        tail = jnp.exp(tot - cm)
        vu = bf(jnp.concatenate([vv, uu], axis=0))
        kb = bf(jnp.concatenate([k * tail, b * tail], axis=0))
        ds = lax.dot_general(vu, kb, (((0,), (0,)), ((), ())), preferred_element_type=F32)
        return s * jnp.exp(tot) + jnp.where(same_head, ds, 0.0)

    return ys, each(new_state, states, v, u, kd, bb, cum, total)


def _scan_kernel(*refs, need_out, has_init):
    it = iter(refs)
    dirs = []
    for _ in range(2):
        dirs.append([next(it) for _ in range(6 if need_out else 5)])
    s0_ref = next(it) if has_init else None
    y_refs = [next(it), next(it)] if need_out else None
    sfin_ref = next(it)
    s_ref = next(it)
    ti = pl.program_id(1)
    nb, _, d = dirs[0][0].shape
    nchunk = d // LANES

    @pl.when(ti == 0)
    def _():
        if has_init:
            s_ref[...] = s0_ref[...]
        else:
            s_ref[...] = jnp.zeros_like(s_ref)

    for dd in range(2):
        refs_d = dirs[dd]

        def per_batch(bidx, carry, dd=dd, refs_d=refs_d):
            sls = [slice(c * LANES, (c + 1) * LANES) for c in range(nchunk)]
            tiles = [[ref[bidx, :, sl] for ref in refs_d] for sl in sls]
            states = [s_ref[dd, bidx, c] for c in range(nchunk)]
            ys, new_states = _wkv_pair_chunks(tiles, states, reverse=(dd == 1), need_out=need_out)
            for c in range(nchunk):
                s_ref[dd, bidx, c] = new_states[c]
                if need_out:
                    y_refs[dd][bidx, :, sls[c]] = ys[c]
            return carry

        lax.fori_loop(0, nb, per_batch, 0)

    @pl.when(ti == pl.num_programs(1) - 1)
    def _():
        sfin_ref[...] = s_ref[...]


def wkv_scan(nkk, v, lw, kd, bb, r, s0):
    b, t, d = v.shape
    tb = min(SCAN_CHUNK, t)
    nt = t // tb
    bg = math.gcd(b, SCAN_BATCH_GROUP)
    need_out = r is not None
    has_init = s0 is not None
    fwd = lambda g, i: (g, i, 0)
    bwd = lambda g, i: (g, nt - 1 - i, 0)
    args, in_specs = [], []
    for dd, imap in enumerate((fwd, bwd)):
        arrs = [nkk, v, lw[dd], kd[dd], bb[dd]] + ([r] if need_out else [])
        args += arrs
        in_specs += [pl.BlockSpec((bg, tb, d), imap)] * len(arrs)
    st_shape = (2, b, d // LANES, LANES, LANES)
    st_spec = pl.BlockSpec((2, bg, d // LANES, LANES, LANES), lambda g, i: (0, g, 0, 0, 0))
    if has_init:
        args.append(s0)
        in_specs.append(st_spec)
    out_specs, out_shape = [], []
    if need_out:
        out_specs += [pl.BlockSpec((bg, tb, d), fwd), pl.BlockSpec((bg, tb, d), bwd)]
        out_shape += [jax.ShapeDtypeStruct((b, t, d), F32)] * 2
    out_specs.append(st_spec)
    out_shape.append(jax.ShapeDtypeStruct(st_shape, F32))
    outs = pl.pallas_call(
        functools.partial(_scan_kernel, need_out=need_out, has_init=has_init),
        grid=(b // bg, nt),
        in_specs=in_specs,
        out_specs=out_specs,
        out_shape=out_shape,
        scratch_shapes=[pltpu.VMEM((2, bg, d // LANES, LANES, LANES), F32)],
        compiler_params=_cp(("arbitrary", "arbitrary")),
        name="wkv_scan",
    )(*args)
    if need_out:
        return outs[0], outs[1], outs[2]
    return None, None, outs[0]


def _readout_kernel(yf_ref, yb_ref, r_ref, v_ref, g_ref, kd0_ref, kd1_ref, rk_ref, gw_ref, gbias_ref,
                    wo_ref, h_ref, mods_ref, o_ref, z_ref):
    ones = _pair_ones()
    d = h_ref.shape[1]
    inv = 1.0 / RWKV_HEAD
    for c in range(d // LANES):
        sl = slice(c * LANES, (c + 1) * LANES)
        y = yf_ref[:, sl] + yb_ref[:, sl]
        mean = _segsum64(y, ones) * inv
        yc = y - mean
        var = _segsum64(yc * yc, ones) * inv
        yn = yc * lax.rsqrt(var + GN_EPS) * gw_ref[:, sl] + gbias_ref[:, sl]
        rkk = r_ref[:, sl] * (kd0_ref[:, sl] + kd1_ref[:, sl]) * rk_ref[:, sl]
        bonus = _segsum64(rkk, ones) * v_ref[:, sl]
        z_ref[:, sl] = ((yn + bonus) * g_ref[:, sl]).astype(BF16)
    out = jnp.dot(z_ref[...], wo_ref[...], preferred_element_type=F32)
    o_ref[...] = h_ref[...] + mods_ref[2:3, :] * out


def rwkv_readout(yf, yb, r, v, g, kd0, kd1, r_k, gn_w, gn_b, w_o, h, mods):
    b, t, d = h.shape
    tr = min(ROW_TILE, t)
    row = lambda bi, si: (bi, si, 0)
    rs = pl.BlockSpec((None, tr, d), row)
    return pl.pallas_call(
        _readout_kernel,
        grid=(b, t // tr),
        in_specs=[rs] * 7 + [_full((1, d))] * 3 + [_full((d, d)), rs,
                                                    pl.BlockSpec((None, 6, d), lambda bi, si: (bi, 0, 0))],
        out_specs=rs,
        out_shape=jax.ShapeDtypeStruct((b, t, d), F32),
        scratch_shapes=[pltpu.VMEM((tr, d), BF16)],
        compiler_params=_cp(("parallel", "parallel")),
        name="rwkv_readout",
    )(yf, yb, r, v, g, kd0, kd1, r_k.reshape(1, d), gn_w.reshape(1, d), gn_b.reshape(1, d), w_o, h, mods)


def _rope_tables(seq_len):
    t = jnp.arange(seq_len, dtype=jnp.int32)
    row = (t // GRID_W).astype(F32)
    col = (t % GRID_W).astype(F32)
    axis_dim = HEAD_DIM // 2
    inv_freq = ROPE_THETA ** (-jnp.arange(0, axis_dim, 2, dtype=F32) / axis_dim)
    ang = jnp.concatenate([row[:, None] * inv_freq, col[:, None] * inv_freq], axis=-1)
    cos = jnp.repeat(jnp.cos(ang), 2, axis=-1)
    sin = jnp.repeat(jnp.sin(ang), 2, axis=-1) * jnp.tile(jnp.array([-1.0, 1.0], F32), axis_dim)
    return jnp.tile(cos, (1, LANES // HEAD_DIM)), jnp.tile(sin, (1, LANES // HEAD_DIM))


def kernel(x, c, ctx, c_ctx, ada_w, ada_b, norm1_g, norm2_g, ev_w_in, ev_conv_w, ev_q_gain, ev_k_gain, ev_w_out,
           od_mu, od_w_r, od_w_k, od_w_v, od_w_o, od_g1, od_g2, od_k_k, od_k_a, od_r_k, od_w0, od_w1, od_w2,
           od_a0, od_a1, od_a2, od_gn_w, od_gn_b, peer_wq, peer_keys, peer_u, peer_v):
    b, s, d = x.shape
    lc = ctx.shape[1]
    depth = ada_w.shape[0]
    bf = lambda a: a.astype(BF16)

    pad_rows = (-(b + 1)) % 8
    cc = jnp.concatenate([c, c_ctx[None, :], jnp.zeros((pad_rows, d), F32)], axis=0)
    mods_all = ada_mods(cc, ada_w, ada_b)

    cos_l, sin_l = _rope_tables(s)
    cos_c = jnp.ones((lc, LANES), F32)
    sin_c = jnp.zeros((lc, LANES), F32)
    rep = LANES // HEAD_DIM

    hx, hc = x, ctx
    for i in range(depth):
        last = i == depth - 1
        j = i // 2
        mods_l = mods_all[i, :b].reshape(b, 6, d)
        mods_c = jnp.broadcast_to(mods_all[i, b].reshape(1, 6, d), (b, 6, d))
        if i % 2 == 0:
            w_in = bf(ev_w_in[j])
            w_out = bf(ev_w_out[j])
            qg = jnp.tile(ev_q_gain[j], rep).reshape(1, LANES)
            kg = jnp.tile(ev_k_gain[j], rep).reshape(1, LANES)
            u_l, gb_l, q_l, k_l, v_l = in_proj(hx, mods_l, norm1_g[i], w_in, qg, kg, cos_l, sin_l)
            u_c, gb_c, q_c, k_c, v_c = in_proj(hc, mods_c, norm1_g[i], w_in, qg, kg, cos_c, sin_c)
            att_l = attention(q_l, [(k_c, v_c), (k_l, v_l)])
            hx = conv_out(u_l, gb_l, att_l, ev_conv_w[j], w_out, hx, mods_l)
            if not last:
                att_c = attention(q_c, [(k_c, v_c)])
                hc = conv_out(u_c, gb_c, att_c, ev_conv_w[j], w_out, hc, mods_c)
        else:
            wts = [od_mu[j], bf(od_w_r[j]), bf(od_w_k[j]), bf(od_w_v[j]), bf(od_g1[j]), bf(od_g2[j]),
                   bf(jnp.concatenate([od_w1[j, 0], od_w1[j, 1]], axis=1)),
                   bf(jnp.concatenate([od_w2[j, 0], od_w2[j, 1]], axis=0)),
                   bf(jnp.concatenate([od_a1[j, 0], od_a1[j, 1]], axis=1)),
                   bf(jnp.concatenate([od_a2[j, 0], od_a2[j, 1]], axis=0)),
                   od_w0[j], od_a0[j], od_k_k[j].reshape(1, d), od_k_a[j].reshape(1, d)]
            xn_c = norm_mod(hc, mods_c, norm1_g[i])
            xn_l = norm_mod(hx, mods_l, norm1_g[i])
            pc = rwkv_proj(xn_c, wts, latent=False, need_rg=not last)
            pl_ = rwkv_proj(xn_l, wts, latent=True, need_rg=True)
            if last:
                v_c, nkk_c, d0c, d1c, kd0c, kd1c, bb0c, bb1c = pc
                r_c = g_c = None
            else:
                r_c, g_c, v_c, nkk_c, d0c, d1c, kd0c, kd1c, bb0c, bb1c = pc
            r_l, g_l, v_l, nkk_l, d0l, d1l, kd0l, kd1l, bb0l, bb1l = pl_
            yf_c, yb_c, s_ctx = wkv_scan(nkk_c, v_c, (d0c, d1c), (kd0c, kd1c), (bb0c, bb1c), r_c, None)
            yf_l, yb_l, _ = wkv_scan(nkk_l, v_l, (d0l, d1l), (kd0l, kd1l), (bb0l, bb1l), r_l, s_ctx)
            r_k = od_r_k[j].reshape(d)
            hx = rwkv_readout(yf_l, yb_l, r_l, v_l, g_l, kd0l, kd1l, r_k, od_gn_w[j], od_gn_b[j],
                              bf(od_w_o[j]), hx, mods_l)
            if not last:
                hc = rwkv_readout(yf_c, yb_c, r_c, v_c, g_c, kd0c, kd1c, r_k, od_gn_w[j], od_gn_b[j],
                                  bf(od_w_o[j]), hc, mods_c)
        wq = bf(peer_wq[i])
        keys = bf(peer_keys[i]).reshape(PEER_HEADS * 2, PEER_NKEYS, PEER_HALF)
        u_tab = bf(peer_u[i])
        v_tab = bf(peer_v[i])
        hx = peer_ffn_residual(hx, mods_l, norm2_g[i], wq, keys, u_tab, v_tab)
        if not last:
            hc = peer_ffn_residual(hc, mods_c, norm2_g[i], wq, keys, u_tab, v_tab)
    return hx
```

```python
import functools
import math

import jax
import jax.numpy as jnp
from jax import lax
from jax.experimental import pallas as pl
from jax.experimental.pallas import tpu as pltpu

F32 = jnp.float32
BF16 = jnp.bfloat16

GRID_W = 64
RMS_EPS = 1e-6
HEAD_DIM = 64
CONV_CH = 512
ATT_Q_HEADS = 8
ATT_KV_HEADS = 2
ATT_GROUP = ATT_Q_HEADS // ATT_KV_HEADS
ATT_SCALE = HEAD_DIM ** -0.5
ROPE_THETA = 10000.0
RWKV_HEAD = 64
GN_EPS = 64e-5
L2_EPS = 1e-12
PEER_HEADS = 8
PEER_NKEYS = 128
PEER_TOPK = 16
PEER_HALF = 128

LANES = 128
SUBLANES = 8
ROW_TILE = 256
ATT_TILE = 256
PROJ_TILE = 128
PEER_TOK_TILE = 256
PEER_EXP_CHUNK = 2048
PEER_SCATTER_UNROLL = 8
SCAN_CHUNK = 64
SCAN_BATCH_GROUP = 4
VMEM_LIMIT = 56 * 1024 * 1024


def _cp(sem, vmem=VMEM_LIMIT):
    return pltpu.CompilerParams(dimension_semantics=sem, vmem_limit_bytes=vmem)


def _full(shape):
    n = len(shape)
    return pl.BlockSpec(shape, lambda *_: (0,) * n)


def _pair_ones():
    r = lax.broadcasted_iota(jnp.int32, (LANES, LANES), 0) // HEAD_DIM
    c = lax.broadcasted_iota(jnp.int32, (LANES, LANES), 1) // HEAD_DIM
    return (r == c).astype(BF16)


def _segsum64(x, ones):
    hi = x.astype(BF16)
    lo = (x - hi.astype(F32)).astype(BF16)
    return (jnp.dot(hi, ones, preferred_element_type=F32)
            + jnp.dot(lo, ones, preferred_element_type=F32))


def _norm_mod(x, g, shift, scale):
    ms = jnp.mean(x * x, axis=-1, keepdims=True)
    return (x * lax.rsqrt(ms + RMS_EPS) * g) * (1.0 + scale) + shift


def _ada_kernel(c_ref, w_ref, b_ref, o_ref):
    c = c_ref[...]
    s = c * jax.nn.sigmoid(c)
    o_ref[...] = jnp.dot(s.astype(BF16), w_ref[...].astype(BF16), preferred_element_type=F32) + b_ref[...]


def ada_mods(cc, ada_w, ada_b):
    depth, d, d6 = ada_w.shape
    r = cc.shape[0]
    nblk = d6 // d
    return pl.pallas_call(
        _ada_kernel,
        grid=(depth, nblk),
        in_specs=[pl.BlockSpec((r, d), lambda i, j: (0, 0)),
                  pl.BlockSpec((None, d, d), lambda i, j: (i, 0, j)),
                  pl.BlockSpec((None, 1, d), lambda i, j: (i, 0, j))],
        out_specs=pl.BlockSpec((None, r, d), lambda i, j: (i, 0, j)),
        out_shape=jax.ShapeDtypeStruct((depth, r, d6), F32),
        compiler_params=_cp(("parallel", "parallel")),
        name="ada_mods",
    )(cc, ada_w, ada_b.reshape(depth, 1, d6))


def _inproj_kernel(h_ref, mods_ref, g_ref, w_ref, qg_ref, kg_ref, cos_ref, sin_ref,
                   u_ref, gb_ref, q_ref, k_ref, v_ref):
    ones = _pair_ones()
    x = h_ref[...]
    xn = _norm_mod(x, g_ref[...], mods_ref[0:1, :], mods_ref[1:2, :])
    y = jnp.dot(xn.astype(BF16), w_ref[...], preferred_element_type=F32)
    c = CONV_CH
    u_ref[...] = y[:, 2 * c:3 * c] * y[:, 0:c]
    gb_ref[...] = y[:, c:2 * c]
    cos = cos_ref[...]
    sin = sin_ref[...]
    lane = lax.broadcasted_iota(jnp.int32, (x.shape[0], LANES), 1)
    even = (lane % 2) == 0

    def qk_norm_rope(z, gain):
        ss = _segsum64(z * z, ones) * (1.0 / HEAD_DIM)
        zn = z * lax.rsqrt(ss + RMS_EPS) * gain
        partner = jnp.where(even, pltpu.roll(zn, LANES - 1, 1), pltpu.roll(zn, 1, 1))
        return zn * cos + partner * sin

    q0 = 3 * c
    for j in range(ATT_Q_HEADS * HEAD_DIM // LANES):
        z = y[:, q0 + j * LANES:q0 + (j + 1) * LANES]
        q_ref[:, j * LANES:(j + 1) * LANES] = qk_norm_rope(z, qg_ref[...]) * ATT_SCALE
    k0 = q0 + ATT_Q_HEADS * HEAD_DIM
    k_ref[...] = qk_norm_rope(y[:, k0:k0 + LANES], kg_ref[...]).astype(BF16)
    v_ref[...] = y[:, k0 + LANES:k0 + 2 * LANES].astype(BF16)


def in_proj(h, mods, g, w_in, q_gain, k_gain, cos_t, sin_t):
    b, t, d = h.shape
    tr = min(ROW_TILE, t)
    nt = t // tr
    proj = w_in.shape[1]
    row = lambda bi, si: (bi, si, 0)
    outs = pl.pallas_call(
        _inproj_kernel,
        grid=(b, nt),
        in_specs=[pl.BlockSpec((None, tr, d), row),
                  pl.BlockSpec((None, 6, d), lambda bi, si: (bi, 0, 0)),
                  _full((1, d)), _full((d, proj)), _full((1, LANES)), _full((1, LANES)),
                  pl.BlockSpec((tr, LANES), lambda bi, si: (si, 0)),
                  pl.BlockSpec((tr, LANES), lambda bi, si: (si, 0))],
        out_specs=[pl.BlockSpec((None, tr, CONV_CH), row),
                   pl.BlockSpec((None, tr, CONV_CH), row),
                   pl.BlockSpec((None, tr, CONV_CH), row),
                   pl.BlockSpec((None, tr, LANES), row),
                   pl.BlockSpec((None, tr, LANES), row)],
        out_shape=[jax.ShapeDtypeStruct((b, t, CONV_CH), F32),
                   jax.ShapeDtypeStruct((b, t, CONV_CH), F32),
                   jax.ShapeDtypeStruct((b, t, CONV_CH), F32),
                   jax.ShapeDtypeStruct((b, t, LANES), BF16),
                   jax.ShapeDtypeStruct((b, t, LANES), BF16)],
        compiler_params=_cp(("parallel", "parallel")),
        name="in_proj",
    )(h, mods, g.reshape(1, d), w_in, q_gain, k_gain, cos_t, sin_t)
    return outs


def _attn_kernel(*refs, n_sets):
    q_ref = refs[0]
    kv = refs[1:1 + 2 * n_sets]
    o_ref = refs[1 + 2 * n_sets]
    tq = q_ref.shape[0]
    lane = lax.broadcasted_iota(jnp.int32, (tq, LANES), 1)
    half = lane // HEAD_DIM
    nt = (((1,), (1,)), ((), ()))
    for cchunk in range(ATT_Q_HEADS * HEAD_DIM // LANES):
        qc = q_ref[:, cchunk * LANES:(cchunk + 1) * LANES]
        out_c = jnp.zeros((tq, LANES), F32)
        for hh in range(2):
            h = 2 * cchunk + hh
            j = h // ATT_GROUP
            qa = qc if hh == j else pltpu.roll(qc, HEAD_DIM, 1)
            qm = jnp.where(half == j, qa, 0.0).astype(BF16)
            ss = [lax.dot_general(qm, kv[2 * i][...], nt, preferred_element_type=F32)
                  for i in range(n_sets)]
            m = ss[0].max(axis=-1, keepdims=True)
            for s in ss[1:]:
                m = jnp.maximum(m, s.max(axis=-1, keepdims=True))
            l = jnp.zeros((tq, 1), F32)
            o = jnp.zeros((tq, LANES), F32)
            for i in range(n_sets):
                p = jnp.exp(ss[i] - m)
                l = l + p.sum(axis=-1, keepdims=True)
                o = o + jnp.dot(p.astype(BF16), kv[2 * i + 1][...], preferred_element_type=F32)
            o = o / l
            oa = o if hh == j else pltpu.roll(o, HEAD_DIM, 1)
            out_c = jnp.where(half == hh, oa, out_c)
        o_ref[:, cchunk * LANES:(cchunk + 1) * LANES] = out_c


def attention(q, kv_sets):
    b, t, qd = q.shape
    tq = min(ATT_TILE, t)
    in_specs = [pl.BlockSpec((None, tq, qd), lambda bi, si: (bi, si, 0))]
    args = [q]
    for k, v in kv_sets:
        ln = k.shape[1]
        in_specs += [pl.BlockSpec((None, ln, LANES), lambda bi, si: (bi, 0, 0))] * 2
        args += [k, v]
    return pl.pallas_call(
        functools.partial(_attn_kernel, n_sets=len(kv_sets)),
        grid=(b, t // tq),
        in_specs=in_specs,
        out_specs=pl.BlockSpec((None, tq, qd), lambda bi, si: (bi, si, 0)),
        out_shape=jax.ShapeDtypeStruct((b, t, qd), F32),
        compiler_params=_cp(("parallel", "parallel")),
        name="attention",
    )(*args)


def _conv_out_kernel(u_ref, up_ref, un_ref, gb_ref, att_ref, cw_ref, wo_ref, h_ref, mods_ref, o_ref):
    si = pl.program_id(1)
    last = pl.num_programs(1) - 1
    u = u_ref[...]
    tr = u.shape[0]
    row = lax.broadcasted_iota(jnp.int32, u.shape, 0)
    prev_row = jnp.where(si == 0, 0.0, up_ref[7:8, :])
    next_row = jnp.where(si == last, 0.0, un_ref[0:1, :])
    u_m1 = jnp.where(row == 0, prev_row, pltpu.roll(u, 1, 0))
    u_p1 = jnp.where(row == tr - 1, next_row, pltpu.roll(u, tr - 1, 0))
    conv = u_m1 * cw_ref[0:1, :] + u * cw_ref[1:2, :] + u_p1 * cw_ref[2:3, :]
    conv = gb_ref[...] * conv
    c = CONV_CH
    y = (jnp.dot(conv.astype(BF16), wo_ref[0:c, :], preferred_element_type=F32)
         + jnp.dot(att_ref[...].astype(BF16), wo_ref[c:2 * c, :], preferred_element_type=F32))
    o_ref[...] = h_ref[...] + mods_ref[2:3, :] * y


def conv_out(u, gb, att, conv_w, w_out, h, mods):
    b, t, d = h.shape
    tr = min(ROW_TILE, t)
    nt = t // tr
    r8 = tr // 8
    n8 = t // 8
    row = lambda bi, si: (bi, si, 0)
    return pl.pallas_call(
        _conv_out_kernel,
        grid=(b, nt),
        in_specs=[pl.BlockSpec((None, tr, CONV_CH), row),
                  pl.BlockSpec((None, 8, CONV_CH), lambda bi, si: (bi, jnp.maximum(si * r8 - 1, 0), 0)),
                  pl.BlockSpec((None, 8, CONV_CH), lambda bi, si: (bi, jnp.minimum((si + 1) * r8, n8 - 1), 0)),
                  pl.BlockSpec((None, tr, CONV_CH), row),
                  pl.BlockSpec((None, tr, CONV_CH), row),
                  _full((3, CONV_CH)), _full((d, d)),
                  pl.BlockSpec((None, tr, d), row),
                  pl.BlockSpec((None, 6, d), lambda bi, si: (bi, 0, 0))],
        out_specs=pl.BlockSpec((None, tr, d), row),
        out_shape=jax.ShapeDtypeStruct((b, t, d), F32),
        compiler_params=_cp(("parallel", "parallel")),
        name="conv_out",
    )(u, u, u, gb, att, conv_w, w_out, h, mods)


def _topk_rows(s, k, pos=None):
    n, t = s.shape
    if pos is None:
        pos = lax.broadcasted_iota(jnp.int32, (n, t), 0).astype(F32)
    krow = lax.broadcasted_iota(jnp.int32, (k, t), 0)
    vals = jnp.zeros((k, t), F32)
    idxs = jnp.zeros((k, t), F32)
    for it in range(k):
        m = jnp.max(s, axis=0, keepdims=True)
        idx = jnp.min(jnp.where(s == m, pos, jnp.inf), axis=0, keepdims=True)
        s = jnp.where(pos == idx, -jnp.inf, s)
        vals = jnp.where(krow == it, m, vals)
        idxs = jnp.where(krow == it, idx, idxs)
    return vals, idxs


PEER_NCAND = PEER_TOPK + (PEER_TOPK // 2 - 1) * SUBLANES + PEER_TOPK // 2


def _router_kernel(h_ref, mods_ref, g_ref, wq_ref, keys_ref, xn_ref, a_ref, b_ref, gate_ref,
                   cand_ref, res_ref):
    x = h_ref[...]
    t = x.shape[0]
    xn = _norm_mod(x, g_ref[...], mods_ref[3:4, :], mods_ref[4:5, :])
    xb = xn.astype(BF16)
    xn_ref[...] = xb
    q = jnp.dot(xb, wq_ref[...], preferred_element_type=F32).astype(BF16)
    nt = (((1,), (1,)), ((), ()))
    kk = PEER_TOPK
    j8 = lax.broadcasted_iota(jnp.int32, (SUBLANES, t), 0)
    j8f = j8.astype(F32)
    cand_pos = jnp.concatenate(
        [lax.broadcasted_iota(jnp.int32, (kk, t), 0).astype(F32)]
        + [j8f + float(i * kk) for i in range(1, kk // 2)]
        + [(j8f + float(kk // 2)) * float(kk)], axis=0)
    ncand = cand_pos.shape[0]
    for h in range(PEER_HEADS):
        tops = []
        for p in range(2):
            g = 2 * h + p
            st = lax.dot_general(keys_ref[g], q[:, g * PEER_HALF:(g + 1) * PEER_HALF], nt,
                                 preferred_element_type=F32)
            tops.append(_topk_rows(st, kk))
        (s1, i1), (s2, i2) = tops
        cand_ref[0:kk, :] = s1[0:1, :] + s2
        for i in range(1, kk // 2):
            r0 = kk + (i - 1) * SUBLANES
            cand_ref[r0:r0 + SUBLANES, :] = jnp.where(j8 < kk // (i + 1), s1[i:i + 1, :] + s2[0:SUBLANES, :],
                                                      -jnp.inf)
        cand_ref[ncand - kk // 2:ncand, :] = s1[kk // 2:kk, :] + s2[0:1, :]
        top_s, pos = _topk_rows(cand_ref[...], kk, cand_pos)
        p1 = jnp.floor(pos * (1.0 / kk))
        p2 = pos - p1 * kk
        a_idx = jnp.zeros((kk, t), F32)
        b_idx = jnp.zeros((kk, t), F32)
        for k1 in range(kk):
            a_idx = a_idx + jnp.where(p1 == k1, i1[k1:k1 + 1, :], 0.0)
            b_idx = b_idx + jnp.where(p2 == k1, i2[k1:k1 + 1, :], 0.0)
        e = jnp.exp(top_s - top_s[0:1, :])
        gate = e / jnp.sum(e, axis=0, keepdims=True)
        res_ref[0, h * kk:(h + 1) * kk, :] = a_idx
        res_ref[1, h * kk:(h + 1) * kk, :] = b_idx
        res_ref[2, h * kk:(h + 1) * kk, :] = gate
    a_ref[...] = res_ref[0].T.astype(jnp.int32)
    b_ref[...] = res_ref[1].T.astype(jnp.int32)
    gate_ref[...] = res_ref[2].T


def peer_router(h, mods, g, wq, keys):
    b, t, d = h.shape
    tr = min(ROW_TILE, t)
    row = lambda bi, si: (bi, si, 0)
    hk = PEER_HEADS * PEER_TOPK
    return pl.pallas_call(
        _router_kernel,
        grid=(b, t // tr),
        in_specs=[pl.BlockSpec((None, tr, d), row),
                  pl.BlockSpec((None, 6, d), lambda bi, si: (bi, 0, 0)),
                  _full((1, d)), _full(wq.shape), _full(keys.shape)],
        out_specs=[pl.BlockSpec((None, tr, d), row),
                   pl.BlockSpec((None, tr, hk), row),
                   pl.BlockSpec((None, tr, hk), row),
                   pl.BlockSpec((None, tr, hk), row)],
        out_shape=[jax.ShapeDtypeStruct((b, t, d), BF16),
                   jax.ShapeDtypeStruct((b, t, hk), jnp.int32),
                   jax.ShapeDtypeStruct((b, t, hk), jnp.int32),
                   jax.ShapeDtypeStruct((b, t, hk), F32)],
        scratch_shapes=[pltpu.VMEM((PEER_NCAND, tr), F32),
                        pltpu.VMEM((3, hk, tr), F32)],
        compiler_params=_cp(("parallel", "parallel")),
        name="peer_router",
    )(h, mods, g.reshape(1, d), wq, keys)


def _peer_dense_kernel(xn_ref, a_ref, b_ref, gate_ref, u_ref, v_ref, h_ref, mods_ref, o_ref,
                       g_ref, acc_ref, w_ref):
    j = pl.program_id(1)
    tt = xn_ref.shape[0]
    nk = PEER_NKEYS
    nt = (((1,), (1,)), ((), ()))

    @pl.when(j == 0)
    def _():
        acc_ref[...] = jnp.zeros_like(acc_ref)
        sub = lax.broadcasted_iota(jnp.int32, (nk, a_ref.shape[1]), 0)

        def per_groups(i0, carry):
            for gi in range(PEER_SCATTER_UNROLL):
                i = i0 * PEER_SCATTER_UNROLL + gi
                rows = pl.ds(pl.multiple_of(i * SUBLANES, SUBLANES), SUBLANES)
                ai8, bi8, gt8 = a_ref[rows, :], b_ref[rows, :], gate_ref[rows, :]
                for k in range(SUBLANES):
                    at = jnp.where(sub == ai8[k:k + 1, :], gt8[k:k + 1, :], 0.0).astype(BF16)
                    bt = (sub == bi8[k:k + 1, :]).astype(BF16)
                    g_ref[i, pl.ds(k, nk, stride=SUBLANES), :] = lax.dot_general(
                        at, bt, nt, preferred_element_type=F32)
            return carry

        lax.fori_loop(0, tt // (SUBLANES * PEER_SCATTER_UNROLL), per_groups, 0)

    ec = u_ref.shape[0]
    halves = [slice(p * (ec // 2), (p + 1) * (ec // 2)) for p in range(2)]
    xb = xn_ref[...]
    hmats = [lax.dot_general(xb, u_ref[sl, :], nt, preferred_element_type=F32) for sl in halves]
    nblk = ec // 2 // nk
    acc = acc_ref[...]
    for p, sl in enumerate(halves):
        for blk in range(nblk):
            hb = hmats[p][:, blk * nk:(blk + 1) * nk]
            act = 0.5 * hb * (1.0 + lax.erf(hb * (1.0 / math.sqrt(2.0))))
            a0 = (j * 2 + p) * nblk + blk
            gate = g_ref[:, pl.ds(pl.multiple_of(a0 * SUBLANES, SUBLANES), SUBLANES), :].reshape(tt, nk)
            w_ref[:, sl.start + blk * nk:sl.start + (blk + 1) * nk] = (gate * act).astype(BF16)
        acc = acc + jnp.dot(w_ref[:, sl], v_ref[sl, :], preferred_element_type=F32)
    acc_ref[...] = acc

    @pl.when(j == pl.num_programs(1) - 1)
    def _():
        o_ref[...] = h_ref[...] + mods_ref[5:6, :] * acc_ref[...]


def peer_dense(xn, a_idx, b_idx, gate, u_tab, v_tab, h, mods):
    b, t, d = h.shape
    n = b * t
    tt = min(PEER_TOK_TILE, t)
    tiles_per_seq = t // tt
    ne = u_tab.shape[0]
    ec = PEER_EXP_CHUNK
    hk = a_idx.shape[-1]
    tok = lambda i, j: (i, 0)
    out = pl.pallas_call(
        _peer_dense_kernel,
        grid=(n // tt, ne // ec),
        in_specs=[pl.BlockSpec((tt, d), tok),
                  pl.BlockSpec((tt, hk), tok), pl.BlockSpec((tt, hk), tok), pl.BlockSpec((tt, hk), tok),
                  pl.BlockSpec((ec, d), lambda i, j: (j, 0)),
                  pl.BlockSpec((ec, d), lambda i, j: (j, 0)),
                  pl.BlockSpec((tt, d), tok),
                  pl.BlockSpec((None, 6, d), lambda i, j: (i // tiles_per_seq, 0, 0))],
        out_specs=pl.BlockSpec((tt, d), tok),
        out_shape=jax.ShapeDtypeStruct((n, d), F32),
        scratch_shapes=[pltpu.VMEM((tt // SUBLANES, PEER_NKEYS * SUBLANES, PEER_NKEYS), F32),
                        pltpu.VMEM((tt, d), F32),
                        pltpu.VMEM((tt, ec), BF16)],
        compiler_params=_cp(("parallel", "arbitrary")),
        name="peer_dense",
    )(xn.reshape(n, d), a_idx.reshape(n, hk), b_idx.reshape(n, hk), gate.reshape(n, hk),
      u_tab, v_tab, h.reshape(n, d), mods)
    return out.reshape(b, t, d)


def peer_ffn_residual(h, mods, g, wq, keys, u_tab, v_tab):
    xn, a_idx, b_idx, gate = peer_router(h, mods, g, wq, keys)
    return peer_dense(xn, a_idx, b_idx, gate, u_tab, v_tab, h, mods)


def _normmod_kernel(h_ref, mods_ref, g_ref, o_ref):
    o_ref[...] = _norm_mod(h_ref[...], g_ref[...], mods_ref[0:1, :], mods_ref[1:2, :])


def norm_mod(h, mods, g):
    b, t, d = h.shape
    tr = min(ROW_TILE, t)
    row = lambda bi, si: (bi, si, 0)
    return pl.pallas_call(
        _normmod_kernel,
        grid=(b, t // tr),
        in_specs=[pl.BlockSpec((None, tr, d), row),
                  pl.BlockSpec((None, 6, d), lambda bi, si: (bi, 0, 0)),
                  _full((1, d))],
        out_specs=pl.BlockSpec((None, tr, d), row),
        out_shape=jax.ShapeDtypeStruct((b, t, d), F32),
        compiler_params=_cp(("parallel", "parallel")),
        name="norm_mod",
    )(h, mods, g.reshape(1, d))


def _rwkv_proj_kernel(x_ref, xp_ref, xnx_ref, mu_ref, wr_ref, wk_ref, wv_ref, g1_ref, g2_ref,
                      w1_ref, w2_ref, a1_ref, a2_ref, w0_ref, a0_ref, kk_ref, ka_ref,
                      *out_and_scratch, latent, need_rg):
    if need_rg:
        (r_ref, g_ref, v_ref, nkk_ref, dec0_ref, dec1_ref, kd0_ref, kd1_ref, bb0_ref, bb1_ref,
         ext_ref, sh_ref) = out_and_scratch
    else:
        (v_ref, nkk_ref, dec0_ref, dec1_ref, kd0_ref, kd1_ref, bb0_ref, bb1_ref,
         ext_ref, sh_ref) = out_and_scratch
    si = pl.program_id(1)
    last = pl.num_programs(1) - 1
    halo = xp_ref.shape[0]
    t, d = x_ref.shape
    x = x_ref[...]
    ext_ref[0:halo, :] = jnp.where(si == 0, 0.0, xp_ref[...])
    ext_ref[halo:halo + t, :] = x
    ext_ref[halo + t:halo + t + halo, :] = jnp.where(si == last, 0.0, xnx_ref[...])
    if latent:
        q = d // 4
        col = lax.broadcasted_iota(jnp.int32, (t, q), 0) % GRID_W
        sh_ref[:, 0:q] = jnp.where(col == 0, 0.0, ext_ref[halo - 1:halo - 1 + t, 0:q])
        sh_ref[:, q:2 * q] = jnp.where(col == GRID_W - 1, 0.0, ext_ref[halo + 1:halo + 1 + t, q:2 * q])
        sh_ref[:, 2 * q:3 * q] = ext_ref[halo - GRID_W:halo - GRID_W + t, 2 * q:3 * q]
        sh_ref[:, 3 * q:4 * q] = ext_ref[halo + GRID_W:halo + GRID_W + t, 3 * q:4 * q]
    else:
        hd = d // 2
        sh_ref[:, 0:hd] = ext_ref[halo - 1:halo - 1 + t, 0:hd]
        sh_ref[:, hd:d] = ext_ref[halo + 1:halo + 1 + t, hd:d]
    xx = sh_ref[...] - x

    def mix(m):
        return (x + xx * mu_ref[m:m + 1, :]).astype(BF16)

    ones = _pair_ones()
    lane = lax.broadcasted_iota(jnp.int32, (t, LANES), 1)
    k = jnp.dot(mix(2), wk_ref[...], preferred_element_type=F32)
    v_ref[...] = jnp.dot(mix(3), wv_ref[...], preferred_element_type=F32)
    if need_rg:
        r_ref[...] = jnp.dot(mix(0), wr_ref[...], preferred_element_type=F32)
        gg = jax.nn.sigmoid(jnp.dot(mix(5), g1_ref[...], preferred_element_type=F32))
        g_ref[...] = jnp.dot(gg.astype(BF16), g2_ref[...], preferred_element_type=F32)
    hw = jnp.tanh(jnp.dot(mix(1), w1_ref[...], preferred_element_type=F32))
    ha = jnp.dot(mix(4), a1_ref[...], preferred_element_type=F32)
    for c in range(d // LANES):
        sl = slice(c * LANES, (c + 1) * LANES)
        kc = k[:, sl] * kk_ref[:, sl]
        nrm = _segsum64(kc * kc, ones)
        kc = kc * lax.rsqrt(nrm + L2_EPS)
        nkk_ref[:, sl] = -kc
    kk = -nkk_ref[...]
    for dd, (dec_ref, kd_ref, bb_ref) in enumerate(((dec0_ref, kd0_ref, bb0_ref), (dec1_ref, kd1_ref, bb1_ref))):
        sel = (lane // (LANES // 2)) == dd
        z = w0_ref[dd:dd + 1, :] + jnp.dot(jnp.where(sel, hw, 0.0).astype(BF16), w2_ref[...],
                                            preferred_element_type=F32)
        nz = -z
        softplus = jnp.maximum(nz, 0.0) + jnp.log(1.0 + jnp.exp(-jnp.abs(nz)))
        logw = -softplus - 0.5
        dec_ref[...] = -jnp.exp(logw)
        a = jax.nn.sigmoid(a0_ref[dd:dd + 1, :] + jnp.dot(jnp.where(sel, ha, 0.0).astype(BF16), a2_ref[...],
                                                          preferred_element_type=F32))
        kd_ref[...] = k * (1.0 + (a - 1.0) * ka_ref[...])
        bb_ref[...] = kk * a


def rwkv_proj(xn, wts, latent, need_rg):
    b, t, d = xn.shape
    tr = min(PROJ_TILE, t)
    nt = t // tr
    halo = GRID_W
    rh = tr // halo
    nh = t // halo
    row = lambda bi, si: (bi, si, 0)
    n_out = 10 if need_rg else 8
    w_specs = [_full(w.shape) for w in wts]
    outs = pl.pallas_call(
        functools.partial(_rwkv_proj_kernel, latent=latent, need_rg=need_rg),
        grid=(b, nt),
        in_specs=[pl.BlockSpec((None, tr, d), row),
                  pl.BlockSpec((None, halo, d), lambda bi, si: (bi, jnp.maximum(si * rh - 1, 0), 0)),
                  pl.BlockSpec((None, halo, d), lambda bi, si: (bi, jnp.minimum((si + 1) * rh, nh - 1), 0))]
                 + w_specs,
        out_specs=[pl.BlockSpec((None, tr, d), row)] * n_out,
        out_shape=[jax.ShapeDtypeStruct((b, t, d), F32)] * n_out,
        scratch_shapes=[pltpu.VMEM((tr + 2 * halo, d), F32), pltpu.VMEM((tr, d), F32)],
        compiler_params=_cp(("parallel", "parallel")),
        name="rwkv_proj",
    )(xn, xn, xn, *wts)
    return outs


def _dot_nt(a, b):
    return lax.dot_general(a, b, (((1,), (1,)), ((), ())), preferred_element_type=F32)


def _wkv_pair_chunks(tiles, states, reverse, need_out):
    npair = len(states)
    c = tiles[0][0].shape[0]
    n2 = 2 * c
    each = lambda f, *ls: [f(*xs) for xs in zip(*ls)]
    dot = lambda a, b: jnp.dot(a, b, preferred_element_type=F32)
    bf = lambda z: z.astype(BF16)
    row = lax.broadcasted_iota(jnp.int32, (c, LANES), 0)
    head0 = lax.broadcasted_iota(jnp.int32, (c, LANES), 1) < RWKV_HEAD
    tr = lax.broadcasted_iota(jnp.int32, (n2, n2), 0)
    tc = lax.broadcasted_iota(jnp.int32, (n2, n2), 1)
    strict = (tr % c < tc % c) if reverse else (tr % c > tc % c)
    ident = jnp.where(tr == tc, 1.0, 0.0)

    def stack(z):
        return bf(jnp.concatenate([jnp.where(head0, z, 0.0), jnp.where(head0, 0.0, z)], axis=0))

    def cumsum(lw):
        cum = lw
        sh = 1
        while sh < c:
            if reverse:
                cum = cum + jnp.where(row < c - sh, pltpu.roll(cum, c - sh, 0), 0.0)
            else:
                cum = cum + jnp.where(row >= sh, pltpu.roll(cum, sh, 0), 0.0)
            sh *= 2
        return cum

    al, v, lw, kd, bb = [[tl[i] for tl in tiles] for i in range(5)]
    cum = each(cumsum, lw)
    total = [cm[0:1, :] if reverse else cm[c - 1:c, :] for cm in cum]
    pinv = [jnp.exp(-cm) for cm in cum]
    a_st = each(lambda a, cm, l: stack(a * jnp.exp(cm - l)), al, cum, lw)
    b_st = each(lambda b, p: stack(b * p), bb, pinv)
    k_st = each(lambda k, p: stack(k * p), kd, pinv)
    v_st = each(stack, v)
    if need_out:
        rbar = each(lambda r, cm: r * jnp.exp(cm), [tl[5] for tl in tiles], cum)
        x_st = each(lambda a, r: jnp.concatenate([a, stack(r)], axis=0), a_st, rbar)
    else:
        x_st = a_st
    g = each(lambda xs, b, k: _dot_nt(xs, jnp.concatenate([b, k], axis=0)), x_st, b_st, k_st)
    mab = [jnp.where(strict, gg[0:n2, 0:n2], 0.0) for gg in g]
    mv = each(lambda gg, vs: dot(bf(jnp.where(strict, gg[0:n2, n2:2 * n2], 0.0)), vs), g, v_st)
    t = [ident + m for m in mab]
    y = mab
    power = 2
    while power < c:
        if power == 2:
            y = each(lambda z: dot(bf(z), bf(z)), y)
        if 2 * power < c:
            z2 = each(lambda tt, z: dot(bf(jnp.concatenate([tt, z], axis=0)), bf(z)), t, y)
            t = each(lambda tt, z: tt + z[0:n2], t, z2)
            y = [z[n2:2 * n2] for z in z2]
        else:
            t = each(lambda tt, z: tt + dot(bf(tt), bf(z)), t, y)
        power *= 2
    tb = each(bf, t)
    wu_st = each(lambda tt, a, m: dot(tt, jnp.concatenate([a, bf(m)], axis=1)), tb, a_st, mv)
    wm = [w[0:c, 0:LANES] + w[c:n2, 0:LANES] for w in wu_st]
    ut = [w[0:c, LANES:2 * LANES] + w[c:n2, LANES:2 * LANES] for w in wu_st]
    sb = each(bf, states)
    if need_out:
        incl = jnp.logical_or(strict, tr % c == tc % c)
        arb = [bf(jnp.where(incl, gg[n2:2 * n2, 0:n2], 0.0)) for gg in g]
        ark = [bf(jnp.where(incl, gg[n2:2 * n2, n2:2 * n2], 0.0)) for gg in g]
        qy_st = each(lambda a, w: dot(a, bf(w)), arb, wu_st)
        qe_st = [q[:, 0:LANES] for q in qy_st]
        yi_st = each(lambda ak, vs, q: dot(ak, vs) + q[:, LANES:2 * LANES], ark, v_st, qy_st)
        qe = each(lambda r, q: r + q[0:c] + q[c:n2], rbar, qe_st)
        uy = each(lambda w, q, s: _dot_nt(bf(jnp.concatenate([w, q], axis=0)), s), wm, qe, sb)
        u = each(lambda z, ut_: z[0:c] + ut_, uy, ut)
        ys = each(lambda z, yi: z[c:n2] + yi[0:c] + yi[c:n2], uy, yi_st)
    else:
        u = each(lambda w, s, ut_: _dot_nt(bf(w), s) + ut_, wm, sb, ut)
        ys = [None] * npair
    same_head = (lax.broadcasted_iota(jnp.int32, (LANES, LANES), 0) // RWKV_HEAD
                 == lax.broadcasted_iota(jnp.int32, (LANES, LANES), 1) // RWKV_HEAD)

    def new_state(s, vv, uu, k, b, cm, tot):
        tail = jnp.exp(tot - cm)
        vu = bf(jnp.concatenate([vv, uu], axis=0))
        kb = bf(jnp.concatenate([k * tail, b * tail], axis=0))
        ds = lax.dot_general(vu, kb, (((0,), (0,)), ((), ())), preferred_element_type=F32)
        return s * jnp.exp(tot) + jnp.where(same_head, ds, 0.0)

    return ys, each(new_state, states, v, u, kd, bb, cum, total)


def _scan_kernel(*refs, need_out, has_init):
    it = iter(refs)
    dirs = []
    for _ in range(2):
        dirs.append([next(it) for _ in range(6 if need_out else 5)])
    s0_ref = next(it) if has_init else None
    y_refs = [next(it), next(it)] if need_out else None
    sfin_ref = next(it)
    s_ref = next(it)
    ti = pl.program_id(1)
    nb, _, d = dirs[0][0].shape
    nchunk = d // LANES

    @pl.when(ti == 0)
    def _():
        if has_init:
            s_ref[...] = s0_ref[...]
        else:
            s_ref[...] = jnp.zeros_like(s_ref)

    for dd in range(2):
        refs_d = dirs[dd]

        def per_batch(bidx, carry, dd=dd, refs_d=refs_d):
            sls = [slice(c * LANES, (c + 1) * LANES) for c in range(nchunk)]
            tiles = [[ref[bidx, :, sl] for ref in refs_d] for sl in sls]
            states = [s_ref[dd, bidx, c] for c in range(nchunk)]
            ys, new_states = _wkv_pair_chunks(tiles, states, reverse=(dd == 1), need_out=need_out)
            for c in range(nchunk):
                s_ref[dd, bidx, c] = new_states[c]
                if need_out:
                    y_refs[dd][bidx, :, sls[c]] = ys[c]
            return carry

        lax.fori_loop(0, nb, per_batch, 0)

    @pl.when(ti == pl.num_programs(1) - 1)
    def _():
        sfin_ref[...] = s_ref[...]


def wkv_scan(nkk, v, lw, kd, bb, r, s0):
    b, t, d = v.shape
    tb = min(SCAN_CHUNK, t)
    nt = t // tb
    bg = math.gcd(b, SCAN_BATCH_GROUP)
    need_out = r is not None
    has_init = s0 is not None
    fwd = lambda g, i: (g, i, 0)
    bwd = lambda g, i: (g, nt - 1 - i, 0)
    args, in_specs = [], []
    for dd, imap in enumerate((fwd, bwd)):
        arrs = [nkk, v, lw[dd], kd[dd], bb[dd]] + ([r] if need_out else [])
        args += arrs
        in_specs += [pl.BlockSpec((bg, tb, d), imap)] * len(arrs)
    st_shape = (2, b, d // LANES, LANES, LANES)
    st_spec = pl.BlockSpec((2, bg, d // LANES, LANES, LANES), lambda g, i: (0, g, 0, 0, 0))
    if has_init:
        args.append(s0)
        in_specs.append(st_spec)
    out_specs, out_shape = [], []
    if need_out:
        out_specs += [pl.BlockSpec((bg, tb, d), fwd), pl.BlockSpec((bg, tb, d), bwd)]
        out_shape += [jax.ShapeDtypeStruct((b, t, d), F32)] * 2
    out_specs.append(st_spec)
    out_shape.append(jax.ShapeDtypeStruct(st_shape, F32))
    outs = pl.pallas_call(
        functools.partial(_scan_kernel, need_out=need_out, has_init=has_init),
        grid=(b // bg, nt),
        in_specs=in_specs,
        out_specs=out_specs,
        out_shape=out_shape,
        scratch_shapes=[pltpu.VMEM((2, bg, d // LANES, LANES, LANES), F32)],
        compiler_params=_cp(("arbitrary", "arbitrary")),
        name="wkv_scan",
    )(*args)
    if need_out:
        return outs[0], outs[1], outs[2]
    return None, None, outs[0]


def _readout_kernel(yf_ref, yb_ref, r_ref, v_ref, g_ref, kd0_ref, kd1_ref, rk_ref, gw_ref, gbias_ref,
                    wo_ref, h_ref, mods_ref, o_ref, z_ref):
    ones = _pair_ones()
    d = h_ref.shape[1]
    inv = 1.0 / RWKV_HEAD
    for c in range(d // LANES):
        sl = slice(c * LANES, (c + 1) * LANES)
        y = yf_ref[:, sl] + yb_ref[:, sl]
        mean = _segsum64(y, ones) * inv
        yc = y - mean
        var = _segsum64(yc * yc, ones) * inv
        yn = yc * lax.rsqrt(var + GN_EPS) * gw_ref[:, sl] + gbias_ref[:, sl]
        rkk = r_ref[:, sl] * (kd0_ref[:, sl] + kd1_ref[:, sl]) * rk_ref[:, sl]
        bonus = _segsum64(rkk, ones) * v_ref[:, sl]
        z_ref[:, sl] = ((yn + bonus) * g_ref[:, sl]).astype(BF16)
    out = jnp.dot(z_ref[...], wo_ref[...], preferred_element_type=F32)
    o_ref[...] = h_ref[...] + mods_ref[2:3, :] * out


def rwkv_readout(yf, yb, r, v, g, kd0, kd1, r_k, gn_w, gn_b, w_o, h, mods):
    b, t, d = h.shape
    tr = min(ROW_TILE, t)
    row = lambda bi, si: (bi, si, 0)
    rs = pl.BlockSpec((None, tr, d), row)
    return pl.pallas_call(
        _readout_kernel,
        grid=(b, t // tr),
        in_specs=[rs] * 7 + [_full((1, d))] * 3 + [_full((d, d)), rs,
                                                    pl.BlockSpec((None, 6, d), lambda bi, si: (bi, 0, 0))],
        out_specs=rs,
        out_shape=jax.ShapeDtypeStruct((b, t, d), F32),
        scratch_shapes=[pltpu.VMEM((tr, d), BF16)],
        compiler_params=_cp(("parallel", "parallel")),
        name="rwkv_readout",
    )(yf, yb, r, v, g, kd0, kd1, r_k.reshape(1, d), gn_w.reshape(1, d), gn_b.reshape(1, d), w_o, h, mods)


def _rope_tables(seq_len):
    t = jnp.arange(seq_len, dtype=jnp.int32)
    row = (t // GRID_W).astype(F32)
    col = (t % GRID_W).astype(F32)
    axis_dim = HEAD_DIM // 2
    inv_freq = ROPE_THETA ** (-jnp.arange(0, axis_dim, 2, dtype=F32) / axis_dim)
    ang = jnp.concatenate([row[:, None] * inv_freq, col[:, None] * inv_freq], axis=-1)
    cos = jnp.repeat(jnp.cos(ang), 2, axis=-1)
    sin = jnp.repeat(jnp.sin(ang), 2, axis=-1) * jnp.tile(jnp.array([-1.0, 1.0], F32), axis_dim)
    return jnp.tile(cos, (1, LANES // HEAD_DIM)), jnp.tile(sin, (1, LANES // HEAD_DIM))


def kernel(x, c, ctx, c_ctx, ada_w, ada_b, norm1_g, norm2_g, ev_w_in, ev_conv_w, ev_q_gain, ev_k_gain, ev_w_out,
           od_mu, od_w_r, od_w_k, od_w_v, od_w_o, od_g1, od_g2, od_k_k, od_k_a, od_r_k, od_w0, od_w1, od_w2,
           od_a0, od_a1, od_a2, od_gn_w, od_gn_b, peer_wq, peer_keys, peer_u, peer_v):
    b, s, d = x.shape
    lc = ctx.shape[1]
    depth = ada_w.shape[0]
    bf = lambda a: a.astype(BF16)

    pad_rows = (-(b + 1)) % 8
    cc = jnp.concatenate([c, c_ctx[None, :], jnp.zeros((pad_rows, d), F32)], axis=0)
    mods_all = ada_mods(cc, ada_w, ada_b)

    cos_l, sin_l = _rope_tables(s)
    cos_c = jnp.ones((lc, LANES), F32)
    sin_c = jnp.zeros((lc, LANES), F32)
    rep = LANES // HEAD_DIM

    hx, hc = x, ctx
    for i in range(depth):
        last = i == depth - 1
        j = i // 2
        mods_l = mods_all[i, :b].reshape(b, 6, d)
        mods_c = jnp.broadcast_to(mods_all[i, b].reshape(1, 6, d), (b, 6, d))
        if i % 2 == 0:
            w_in = bf(ev_w_in[j])
            w_out = bf(ev_w_out[j])
            qg = jnp.tile(ev_q_gain[j], rep).reshape(1, LANES)
            kg = jnp.tile(ev_k_gain[j], rep).reshape(1, LANES)
            u_l, gb_l, q_l, k_l, v_l = in_proj(hx, mods_l, norm1_g[i], w_in, qg, kg, cos_l, sin_l)
            u_c, gb_c, q_c, k_c, v_c = in_proj(hc, mods_c, norm1_g[i], w_in, qg, kg, cos_c, sin_c)
            att_l = attention(q_l, [(k_c, v_c), (k_l, v_l)])
            hx = conv_out(u_l, gb_l, att_l, ev_conv_w[j], w_out, hx, mods_l)
            if not last:
                att_c = attention(q_c, [(k_c, v_c)])
                hc = conv_out(u_c, gb_c, att_c, ev_conv_w[j], w_out, hc, mods_c)
        else:
            wts = [od_mu[j], bf(od_w_r[j]), bf(od_w_k[j]), bf(od_w_v[j]), bf(od_g1[j]), bf(od_g2[j]),
                   bf(jnp.concatenate([od_w1[j, 0], od_w1[j, 1]], axis=1)),
                   bf(jnp.concatenate([od_w2[j, 0], od_w2[j, 1]], axis=0)),
                   bf(jnp.concatenate([od_a1[j, 0], od_a1[j, 1]], axis=1)),
                   bf(jnp.concatenate([od_a2[j, 0], od_a2[j, 1]], axis=0)),
                   od_w0[j], od_a0[j], od_k_k[j].reshape(1, d), od_k_a[j].reshape(1, d)]
            xn_c = norm_mod(hc, mods_c, norm1_g[i])
            xn_l = norm_mod(hx, mods_l, norm1_g[i])
            pc = rwkv_proj(xn_c, wts, latent=False, need_rg=not last)
            pl_ = rwkv_proj(xn_l, wts, latent=True, need_rg=True)
            if last:
                v_c, nkk_c, d0c, d1c, kd0c, kd1c, bb0c, bb1c = pc
                r_c = g_c = None
            else:
                r_c, g_c, v_c, nkk_c, d0c, d1c, kd0c, kd1c, bb0c, bb1c = pc
            r_l, g_l, v_l, nkk_l, d0l, d1l, kd0l, kd1l, bb0l, bb1l = pl_
            yf_c, yb_c, s_ctx = wkv_scan(nkk_c, v_c, (d0c, d1c), (kd0c, kd1c), (bb0c, bb1c), r_c, None)
            yf_l, yb_l, _ = wkv_scan(nkk_l, v_l, (d0l, d1l), (kd0l, kd1l), (bb0l, bb1l), r_l, s_ctx)
            r_k = od_r_k[j].reshape(d)
            hx = rwkv_readout(yf_l, yb_l, r_l, v_l, g_l, kd0l, kd1l, r_k, od_gn_w[j], od_gn_b[j],
                              bf(od_w_o[j]), hx, mods_l)
            if not last:
                hc = rwkv_readout(yf_c, yb_c, r_c, v_c, g_c, kd0c, kd1c, r_k, od_gn_w[j], od_gn_b[j],
                                  bf(od_w_o[j]), hc, mods_c)
        wq = bf(peer_wq[i])
        keys = bf(peer_keys[i]).reshape(PEER_HEADS * 2, PEER_NKEYS, PEER_HALF)
        u_tab = bf(peer_u[i])
        v_tab = bf(peer_v[i])
        hx = peer_ffn_residual(hx, mods_l, norm2_g[i], wq, keys, u_tab, v_tab)
        if not last:
            hc = peer_ffn_residual(hc, mods_c, norm2_g[i], wq, keys, u_tab, v_tab)
    return hx
```

```python
import functools
import math

import jax
import jax.numpy as jnp
from jax import lax
from jax.experimental import pallas as pl
from jax.experimental.pallas import tpu as pltpu

F32 = jnp.float32
BF16 = jnp.bfloat16

GRID_W = 64
RMS_EPS = 1e-6
HEAD_DIM = 64
CONV_CH = 512
ATT_Q_HEADS = 8
ATT_KV_HEADS = 2
ATT_GROUP = ATT_Q_HEADS // ATT_KV_HEADS
ATT_SCALE = HEAD_DIM ** -0.5
ROPE_THETA = 10000.0
RWKV_HEAD = 64
GN_EPS = 64e-5
L2_EPS = 1e-12
PEER_HEADS = 8
PEER_NKEYS = 128
PEER_TOPK = 16
PEER_HALF = 128

LANES = 128
SUBLANES = 8
ROW_TILE = 256
ATT_TILE = 256
PROJ_TILE = 128
PEER_TOK_TILE = 256
PEER_EXP_CHUNK = 2048
PEER_SCATTER_UNROLL = 16
SCAN_CHUNK = 64
SCAN_BATCH_GROUP = 4
SCAN_DIR_GROUPS = ((0, 1),)
VMEM_LIMIT = 56 * 1024 * 1024


def _cp(sem, vmem=VMEM_LIMIT):
    return pltpu.CompilerParams(dimension_semantics=sem, vmem_limit_bytes=vmem)


def _full(shape):
    n = len(shape)
    return pl.BlockSpec(shape, lambda *_: (0,) * n)


def _pair_ones():
    r = lax.broadcasted_iota(jnp.int32, (LANES, LANES), 0) // HEAD_DIM
    c = lax.broadcasted_iota(jnp.int32, (LANES, LANES), 1) // HEAD_DIM
    return (r == c).astype(BF16)


def _segsum64(x, ones):
    hi = x.astype(BF16)
    lo = (x - hi.astype(F32)).astype(BF16)
    return (jnp.dot(hi, ones, preferred_element_type=F32)
            + jnp.dot(lo, ones, preferred_element_type=F32))


def _norm_mod(x, g, shift, scale):
    ms = jnp.mean(x * x, axis=-1, keepdims=True)
    return (x * lax.rsqrt(ms + RMS_EPS) * g) * (1.0 + scale) + shift


def _ada_kernel(c_ref, w_ref, b_ref, o_ref):
    c = c_ref[...]
    s = c * jax.nn.sigmoid(c)
    o_ref[...] = jnp.dot(s.astype(BF16), w_ref[...].astype(BF16), preferred_element_type=F32) + b_ref[...]


def ada_mods(cc, ada_w, ada_b):
    depth, d, d6 = ada_w.shape
    r = cc.shape[0]
    nblk = d6 // d
    return pl.pallas_call(
        _ada_kernel,
        grid=(depth, nblk),
        in_specs=[pl.BlockSpec((r, d), lambda i, j: (0, 0)),
                  pl.BlockSpec((None, d, d), lambda i, j: (i, 0, j)),
                  pl.BlockSpec((None, 1, d), lambda i, j: (i, 0, j))],
        out_specs=pl.BlockSpec((None, r, d), lambda i, j: (i, 0, j)),
        out_shape=jax.ShapeDtypeStruct((depth, r, d6), F32),
        compiler_params=_cp(("parallel", "parallel")),
        name="ada_mods",
    )(cc, ada_w, ada_b.reshape(depth, 1, d6))


def _inproj_kernel(h_ref, mods_ref, g_ref, w_ref, qg_ref, kg_ref, cos_ref, sin_ref,
                   u_ref, gb_ref, q_ref, k_ref, v_ref):
    ones = _pair_ones()
    x = h_ref[...]
    xn = _norm_mod(x, g_ref[...], mods_ref[0:1, :], mods_ref[1:2, :])
    y = jnp.dot(xn.astype(BF16), w_ref[...], preferred_element_type=F32)
    c = CONV_CH
    u_ref[...] = y[:, 2 * c:3 * c] * y[:, 0:c]
    gb_ref[...] = y[:, c:2 * c]
    cos = cos_ref[...]
    sin = sin_ref[...]
    lane = lax.broadcasted_iota(jnp.int32, (x.shape[0], LANES), 1)
    even = (lane % 2) == 0

    def qk_norm_rope(z, gain):
        ss = _segsum64(z * z, ones) * (1.0 / HEAD_DIM)
        zn = z * lax.rsqrt(ss + RMS_EPS) * gain
        partner = jnp.where(even, pltpu.roll(zn, LANES - 1, 1), pltpu.roll(zn, 1, 1))
        return zn * cos + partner * sin

    q0 = 3 * c
    for j in range(ATT_Q_HEADS * HEAD_DIM // LANES):
        z = y[:, q0 + j * LANES:q0 + (j + 1) * LANES]
        q_ref[:, j * LANES:(j + 1) * LANES] = qk_norm_rope(z, qg_ref[...]) * ATT_SCALE
    k0 = q0 + ATT_Q_HEADS * HEAD_DIM
    k_ref[...] = qk_norm_rope(y[:, k0:k0 + LANES], kg_ref[...]).astype(BF16)
    v_ref[...] = y[:, k0 + LANES:k0 + 2 * LANES].astype(BF16)


def in_proj(h, mods, g, w_in, q_gain, k_gain, cos_t, sin_t):
    b, t, d = h.shape
    tr = min(ROW_TILE, t)
    nt = t // tr
    proj = w_in.shape[1]
    row = lambda bi, si: (bi, si, 0)
    outs = pl.pallas_call(
        _inproj_kernel,
        grid=(b, nt),
        in_specs=[pl.BlockSpec((None, tr, d), row),
                  pl.BlockSpec((None, 6, d), lambda bi, si: (bi, 0, 0)),
                  _full((1, d)), _full((d, proj)), _full((1, LANES)), _full((1, LANES)),
                  pl.BlockSpec((tr, LANES), lambda bi, si: (si, 0)),
                  pl.BlockSpec((tr, LANES), lambda bi, si: (si, 0))],
        out_specs=[pl.BlockSpec((None, tr, CONV_CH), row),
                   pl.BlockSpec((None, tr, CONV_CH), row),
                   pl.BlockSpec((None, tr, CONV_CH), row),
                   pl.BlockSpec((None, tr, LANES), row),
                   pl.BlockSpec((None, tr, LANES), row)],
        out_shape=[jax.ShapeDtypeStruct((b, t, CONV_CH), F32),
                   jax.ShapeDtypeStruct((b, t, CONV_CH), F32),
                   jax.ShapeDtypeStruct((b, t, CONV_CH), F32),
                   jax.ShapeDtypeStruct((b, t, LANES), BF16),
                   jax.ShapeDtypeStruct((b, t, LANES), BF16)],
        compiler_params=_cp(("parallel", "parallel")),
        name="in_proj",
    )(h, mods, g.reshape(1, d), w_in, q_gain, k_gain, cos_t, sin_t)
    return outs


def _attn_kernel(*refs, n_sets):
    q_ref = refs[0]
    kv = refs[1:1 + 2 * n_sets]
    o_ref = refs[1 + 2 * n_sets]
    tq = q_ref.shape[0]
    lane = lax.broadcasted_iota(jnp.int32, (tq, LANES), 1)
    half = lane // HEAD_DIM
    nt = (((1,), (1,)), ((), ()))
    for cchunk in range(ATT_Q_HEADS * HEAD_DIM // LANES):
        qc = q_ref[:, cchunk * LANES:(cchunk + 1) * LANES]
        out_c = jnp.zeros((tq, LANES), F32)
        for hh in range(2):
            h = 2 * cchunk + hh
            j = h // ATT_GROUP
            qa = qc if hh == j else pltpu.roll(qc, HEAD_DIM, 1)
            qm = jnp.where(half == j, qa, 0.0).astype(BF16)
            ss = [lax.dot_general(qm, kv[2 * i][...], nt, preferred_element_type=F32)
                  for i in range(n_sets)]
            m = ss[0].max(axis=-1, keepdims=True)
            for s in ss[1:]:
                m = jnp.maximum(m, s.max(axis=-1, keepdims=True))
            l = jnp.zeros((tq, 1), F32)
            o = jnp.zeros((tq, LANES), F32)
            for i in range(n_sets):
                p = jnp.exp(ss[i] - m)
                l = l + p.sum(axis=-1, keepdims=True)
                o = o + jnp.dot(p.astype(BF16), kv[2 * i + 1][...], preferred_element_type=F32)
            o = o / l
            oa = o if hh == j else pltpu.roll(o, HEAD_DIM, 1)
            out_c = jnp.where(half == hh, oa, out_c)
        o_ref[:, cchunk * LANES:(cchunk + 1) * LANES] = out_c


def attention(q, kv_sets):
    b, t, qd = q.shape
    tq = min(ATT_TILE, t)
    in_specs = [pl.BlockSpec((None, tq, qd), lambda bi, si: (bi, si, 0))]
    args = [q]
    for k, v in kv_sets:
        ln = k.shape[1]
        in_specs += [pl.BlockSpec((None, ln, LANES), lambda bi, si: (bi, 0, 0))] * 2
        args += [k, v]
    return pl.pallas_call(
        functools.partial(_attn_kernel, n_sets=len(kv_sets)),
        grid=(b, t // tq),
        in_specs=in_specs,
        out_specs=pl.BlockSpec((None, tq, qd), lambda bi, si: (bi, si, 0)),
        out_shape=jax.ShapeDtypeStruct((b, t, qd), F32),
        compiler_params=_cp(("parallel", "parallel")),
        name="attention",
    )(*args)


def _conv_out_kernel(u_ref, up_ref, un_ref, gb_ref, att_ref, cw_ref, wo_ref, h_ref, mods_ref, o_ref):
    si = pl.program_id(1)
    last = pl.num_programs(1) - 1
    u = u_ref[...]
    tr = u.shape[0]
    row = lax.broadcasted_iota(jnp.int32, u.shape, 0)
    prev_row = jnp.where(si == 0, 0.0, up_ref[7:8, :])
    next_row = jnp.where(si == last, 0.0, un_ref[0:1, :])
    u_m1 = jnp.where(row == 0, prev_row, pltpu.roll(u, 1, 0))
    u_p1 = jnp.where(row == tr - 1, next_row, pltpu.roll(u, tr - 1, 0))
    conv = u_m1 * cw_ref[0:1, :] + u * cw_ref[1:2, :] + u_p1 * cw_ref[2:3, :]
    conv = gb_ref[...] * conv
    c = CONV_CH
    y = (jnp.dot(conv.astype(BF16), wo_ref[0:c, :], preferred_element_type=F32)
         + jnp.dot(att_ref[...].astype(BF16), wo_ref[c:2 * c, :], preferred_element_type=F32))
    o_ref[...] = h_ref[...] + mods_ref[2:3, :] * y


def conv_out(u, gb, att, conv_w, w_out, h, mods):
    b, t, d = h.shape
    tr = min(ROW_TILE, t)
    nt = t // tr
    r8 = tr // 8
    n8 = t // 8
    row = lambda bi, si: (bi, si, 0)
    return pl.pallas_call(
        _conv_out_kernel,
        grid=(b, nt),
        in_specs=[pl.BlockSpec((None, tr, CONV_CH), row),
                  pl.BlockSpec((None, 8, CONV_CH), lambda bi, si: (bi, jnp.maximum(si * r8 - 1, 0), 0)),
                  pl.BlockSpec((None, 8, CONV_CH), lambda bi, si: (bi, jnp.minimum((si + 1) * r8, n8 - 1), 0)),
                  pl.BlockSpec((None, tr, CONV_CH), row),
                  pl.BlockSpec((None, tr, CONV_CH), row),
                  _full((3, CONV_CH)), _full((d, d)),
                  pl.BlockSpec((None, tr, d), row),
                  pl.BlockSpec((None, 6, d), lambda bi, si: (bi, 0, 0))],
        out_specs=pl.BlockSpec((None, tr, d), row),
        out_shape=jax.ShapeDtypeStruct((b, t, d), F32),
        compiler_params=_cp(("parallel", "parallel")),
        name="conv_out",
    )(u, u, u, gb, att, conv_w, w_out, h, mods)


def _topk_rows(s, k, pos=None):
    n, t = s.shape
    if pos is None:
        pos = lax.broadcasted_iota(jnp.int32, (n, t), 0).astype(F32)
    krow = lax.broadcasted_iota(jnp.int32, (k, t), 0)
    vals = jnp.zeros((k, t), F32)
    idxs = jnp.zeros((k, t), F32)
    for it in range(k):
        m = jnp.max(s, axis=0, keepdims=True)
        idx = jnp.min(jnp.where(s == m, pos, jnp.inf), axis=0, keepdims=True)
        s = jnp.where(pos == idx, -jnp.inf, s)
        vals = jnp.where(krow == it, m, vals)
        idxs = jnp.where(krow == it, idx, idxs)
    return vals, idxs


PEER_NCAND = PEER_TOPK + (PEER_TOPK // 2 - 1) * SUBLANES + PEER_TOPK // 2


def _router_kernel(h_ref, mods_ref, g_ref, wq_ref, keys_ref, xn_ref, a_ref, b_ref, gate_ref,
                   cand_ref, res_ref):
    x = h_ref[...]
    t = x.shape[0]
    xn = _norm_mod(x, g_ref[...], mods_ref[3:4, :], mods_ref[4:5, :])
    xb = xn.astype(BF16)
    xn_ref[...] = xb
    q = jnp.dot(xb, wq_ref[...], preferred_element_type=F32).astype(BF16)
    nt = (((1,), (1,)), ((), ()))
    kk = PEER_TOPK
    j8 = lax.broadcasted_iota(jnp.int32, (SUBLANES, t), 0)
    j8f = j8.astype(F32)
    cand_pos = jnp.concatenate(
        [lax.broadcasted_iota(jnp.int32, (kk, t), 0).astype(F32)]
        + [j8f + float(i * kk) for i in range(1, kk // 2)]
        + [(j8f + float(kk // 2)) * float(kk)], axis=0)
    ncand = cand_pos.shape[0]
    for h in range(PEER_HEADS):
        tops = []
        for p in range(2):
            g = 2 * h + p
            st = lax.dot_general(keys_ref[g], q[:, g * PEER_HALF:(g + 1) * PEER_HALF], nt,
                                 preferred_element_type=F32)
            tops.append(_topk_rows(st, kk))
        (s1, i1), (s2, i2) = tops
        cand_ref[0:kk, :] = s1[0:1, :] + s2
        for i in range(1, kk // 2):
            r0 = kk + (i - 1) * SUBLANES
            cand_ref[r0:r0 + SUBLANES, :] = jnp.where(j8 < kk // (i + 1), s1[i:i + 1, :] + s2[0:SUBLANES, :],
                                                      -jnp.inf)
        cand_ref[ncand - kk // 2:ncand, :] = s1[kk // 2:kk, :] + s2[0:1, :]
        top_s, pos = _topk_rows(cand_ref[...], kk, cand_pos)
        p1 = jnp.floor(pos * (1.0 / kk))
        p2 = pos - p1 * kk
        a_idx = jnp.zeros((kk, t), F32)
        b_idx = jnp.zeros((kk, t), F32)
        for k1 in range(kk):
            a_idx = a_idx + jnp.where(p1 == k1, i1[k1:k1 + 1, :], 0.0)
            b_idx = b_idx + jnp.where(p2 == k1, i2[k1:k1 + 1, :], 0.0)
        e = jnp.exp(top_s - top_s[0:1, :])
        gate = e / jnp.sum(e, axis=0, keepdims=True)
        res_ref[0, h * kk:(h + 1) * kk, :] = a_idx
        res_ref[1, h * kk:(h + 1) * kk, :] = b_idx
        res_ref[2, h * kk:(h + 1) * kk, :] = gate
    a_ref[...] = res_ref[0].T.astype(jnp.int32)
    b_ref[...] = res_ref[1].T.astype(jnp.int32)
    gate_ref[...] = res_ref[2].T


def peer_router(h, mods, g, wq, keys):
    b, t, d = h.shape
    tr = min(ROW_TILE, t)
    row = lambda bi, si: (bi, si, 0)
    hk = PEER_HEADS * PEER_TOPK
    return pl.pallas_call(
        _router_kernel,
        grid=(b, t // tr),
        in_specs=[pl.BlockSpec((None, tr, d), row),
                  pl.BlockSpec((None, 6, d), lambda bi, si: (bi, 0, 0)),
                  _full((1, d)), _full(wq.shape), _full(keys.shape)],
        out_specs=[pl.BlockSpec((None, tr, d), row),
                   pl.BlockSpec((None, tr, hk), row),
                   pl.BlockSpec((None, tr, hk), row),
                   pl.BlockSpec((None, tr, hk), row)],
        out_shape=[jax.ShapeDtypeStruct((b, t, d), BF16),
                   jax.ShapeDtypeStruct((b, t, hk), jnp.int32),
                   jax.ShapeDtypeStruct((b, t, hk), jnp.int32),
                   jax.ShapeDtypeStruct((b, t, hk), F32)],
        scratch_shapes=[pltpu.VMEM((PEER_NCAND, tr), F32),
                        pltpu.VMEM((3, hk, tr), F32)],
        compiler_params=_cp(("parallel", "parallel")),
        name="peer_router",
    )(h, mods, g.reshape(1, d), wq, keys)


def _peer_dense_kernel(xn_ref, a_ref, b_ref, gate_ref, u_ref, v_ref, h_ref, mods_ref, o_ref,
                       g_ref, acc_ref, w_ref):
    j = pl.program_id(1)
    tt = xn_ref.shape[0]
    nk = PEER_NKEYS
    nt = (((1,), (1,)), ((), ()))

    @pl.when(j == 0)
    def _():
        acc_ref[...] = jnp.zeros_like(acc_ref)
        sub = lax.broadcasted_iota(jnp.int32, (nk, a_ref.shape[1]), 0)

        def per_groups(i0, carry):
            for gi in range(PEER_SCATTER_UNROLL):
                i = i0 * PEER_SCATTER_UNROLL + gi
                rows = pl.ds(pl.multiple_of(i * SUBLANES, SUBLANES), SUBLANES)
                ai8, bi8, gt8 = a_ref[rows, :], b_ref[rows, :], gate_ref[rows, :]
                for k in range(SUBLANES):
                    at = jnp.where(sub == ai8[k:k + 1, :], gt8[k:k + 1, :], 0.0).astype(BF16)
                    bt = (sub == bi8[k:k + 1, :]).astype(BF16)
                    g_ref[i, pl.ds(k, nk, stride=SUBLANES), :] = lax.dot_general(
                        at, bt, nt, preferred_element_type=F32)
            return carry

        lax.fori_loop(0, tt // (SUBLANES * PEER_SCATTER_UNROLL), per_groups, 0)

    ec = u_ref.shape[0]
    halves = [slice(p * (ec // 2), (p + 1) * (ec // 2)) for p in range(2)]
    xb = xn_ref[...]
    hmats = [lax.dot_general(xb, u_ref[sl, :], nt, preferred_element_type=F32) for sl in halves]
    nblk = ec // 2 // nk
    acc = acc_ref[...]
    for p, sl in enumerate(halves):
        for blk in range(nblk):
            hb = hmats[p][:, blk * nk:(blk + 1) * nk]
            act = 0.5 * hb * (1.0 + lax.erf(hb * (1.0 / math.sqrt(2.0))))
            a0 = (j * 2 + p) * nblk + blk
            gate = g_ref[:, pl.ds(pl.multiple_of(a0 * SUBLANES, SUBLANES), SUBLANES), :].reshape(tt, nk)
            w_ref[:, sl.start + blk * nk:sl.start + (blk + 1) * nk] = (gate * act).astype(BF16)
        acc = acc + jnp.dot(w_ref[:, sl], v_ref[sl, :], preferred_element_type=F32)
    acc_ref[...] = acc

    @pl.when(j == pl.num_programs(1) - 1)
    def _():
        o_ref[...] = h_ref[...] + mods_ref[5:6, :] * acc_ref[...]


def peer_dense(xn, a_idx, b_idx, gate, u_tab, v_tab, h, mods):
    b, t, d = h.shape
    n = b * t
    tt = min(PEER_TOK_TILE, t)
    tiles_per_seq = t // tt
    ne = u_tab.shape[0]
    ec = PEER_EXP_CHUNK
    hk = a_idx.shape[-1]
    tok = lambda i, j: (i, 0)
    out = pl.pallas_call(
        _peer_dense_kernel,
        grid=(n // tt, ne // ec),
        in_specs=[pl.BlockSpec((tt, d), tok),
                  pl.BlockSpec((tt, hk), tok), pl.BlockSpec((tt, hk), tok), pl.BlockSpec((tt, hk), tok),
                  pl.BlockSpec((ec, d), lambda i, j: (j, 0)),
                  pl.BlockSpec((ec, d), lambda i, j: (j, 0)),
                  pl.BlockSpec((tt, d), tok),
                  pl.BlockSpec((None, 6, d), lambda i, j: (i // tiles_per_seq, 0, 0))],
        out_specs=pl.BlockSpec((tt, d), tok),
        out_shape=jax.ShapeDtypeStruct((n, d), F32),
        scratch_shapes=[pltpu.VMEM((tt // SUBLANES, PEER_NKEYS * SUBLANES, PEER_NKEYS), F32),
                        pltpu.VMEM((tt, d), F32),
                        pltpu.VMEM((tt, ec), BF16)],
        compiler_params=_cp(("parallel", "arbitrary")),
        name="peer_dense",
    )(xn.reshape(n, d), a_idx.reshape(n, hk), b_idx.reshape(n, hk), gate.reshape(n, hk),
      u_tab, v_tab, h.reshape(n, d), mods)
    return out.reshape(b, t, d)


def peer_ffn_residual(h, mods, g, wq, keys, u_tab, v_tab):
    xn, a_idx, b_idx, gate = peer_router(h, mods, g, wq, keys)
    return peer_dense(xn, a_idx, b_idx, gate, u_tab, v_tab, h, mods)


def _normmod_kernel(h_ref, mods_ref, g_ref, o_ref):
    o_ref[...] = _norm_mod(h_ref[...], g_ref[...], mods_ref[0:1, :], mods_ref[1:2, :])


def norm_mod(h, mods, g):
    b, t, d = h.shape
    tr = min(ROW_TILE, t)
    row = lambda bi, si: (bi, si, 0)
    return pl.pallas_call(
        _normmod_kernel,
        grid=(b, t // tr),
        in_specs=[pl.BlockSpec((None, tr, d), row),
                  pl.BlockSpec((None, 6, d), lambda bi, si: (bi, 0, 0)),
                  _full((1, d))],
        out_specs=pl.BlockSpec((None, tr, d), row),
        out_shape=jax.ShapeDtypeStruct((b, t, d), F32),
        compiler_params=_cp(("parallel", "parallel")),
        name="norm_mod",
    )(h, mods, g.reshape(1, d))


def _rwkv_proj_kernel(x_ref, xp_ref, xnx_ref, mu_ref, wr_ref, wk_ref, wv_ref, g1_ref, g2_ref,
                      w1_ref, w2_ref, a1_ref, a2_ref, w0_ref, a0_ref, kk_ref, ka_ref,
                      *out_and_scratch, latent, need_rg):
    if need_rg:
        (r_ref, g_ref, v_ref, nkk_ref, dec0_ref, dec1_ref, kd0_ref, kd1_ref, bb0_ref, bb1_ref,
         ext_ref, sh_ref) = out_and_scratch
    else:
        (v_ref, nkk_ref, dec0_ref, dec1_ref, kd0_ref, kd1_ref, bb0_ref, bb1_ref,
         ext_ref, sh_ref) = out_and_scratch
    si = pl.program_id(1)
    last = pl.num_programs(1) - 1
    halo = xp_ref.shape[0]
    t, d = x_ref.shape
    x = x_ref[...]
    ext_ref[0:halo, :] = jnp.where(si == 0, 0.0, xp_ref[...])
    ext_ref[halo:halo + t, :] = x
    ext_ref[halo + t:halo + t + halo, :] = jnp.where(si == last, 0.0, xnx_ref[...])
    if latent:
        q = d // 4
        col = lax.broadcasted_iota(jnp.int32, (t, q), 0) % GRID_W
        sh_ref[:, 0:q] = jnp.where(col == 0, 0.0, ext_ref[halo - 1:halo - 1 + t, 0:q])
        sh_ref[:, q:2 * q] = jnp.where(col == GRID_W - 1, 0.0, ext_ref[halo + 1:halo + 1 + t, q:2 * q])
        sh_ref[:, 2 * q:3 * q] = ext_ref[halo - GRID_W:halo - GRID_W + t, 2 * q:3 * q]
        sh_ref[:, 3 * q:4 * q] = ext_ref[halo + GRID_W:halo + GRID_W + t, 3 * q:4 * q]
    else:
        hd = d // 2
        sh_ref[:, 0:hd] = ext_ref[halo - 1:halo - 1 + t, 0:hd]
        sh_ref[:, hd:d] = ext_ref[halo + 1:halo + 1 + t, hd:d]
    xx = sh_ref[...] - x

    def mix(m):
        return (x + xx * mu_ref[m:m + 1, :]).astype(BF16)

    ones = _pair_ones()
    lane = lax.broadcasted_iota(jnp.int32, (t, LANES), 1)
    k = jnp.dot(mix(2), wk_ref[...], preferred_element_type=F32)
    v_ref[...] = jnp.dot(mix(3), wv_ref[...], preferred_element_type=F32)
    if need_rg:
        r_ref[...] = jnp.dot(mix(0), wr_ref[...], preferred_element_type=F32)
        gg = jax.nn.sigmoid(jnp.dot(mix(5), g1_ref[...], preferred_element_type=F32))
        g_ref[...] = jnp.dot(gg.astype(BF16), g2_ref[...], preferred_element_type=F32)
    hw = jnp.tanh(jnp.dot(mix(1), w1_ref[...], preferred_element_type=F32))
    ha = jnp.dot(mix(4), a1_ref[...], preferred_element_type=F32)
    for c in range(d // LANES):
        sl = slice(c * LANES, (c + 1) * LANES)
        kc = k[:, sl] * kk_ref[:, sl]
        nrm = _segsum64(kc * kc, ones)
        kc = kc * lax.rsqrt(nrm + L2_EPS)
        nkk_ref[:, sl] = -kc
    kk = -nkk_ref[...]
    for dd, (dec_ref, kd_ref, bb_ref) in enumerate(((dec0_ref, kd0_ref, bb0_ref), (dec1_ref, kd1_ref, bb1_ref))):
        sel = (lane // (LANES // 2)) == dd
        z = w0_ref[dd:dd + 1, :] + jnp.dot(jnp.where(sel, hw, 0.0).astype(BF16), w2_ref[...],
                                            preferred_element_type=F32)
        nz = -z
        softplus = jnp.maximum(nz, 0.0) + jnp.log(1.0 + jnp.exp(-jnp.abs(nz)))
        logw = -softplus - 0.5
        dec_ref[...] = -jnp.exp(logw)
        a = jax.nn.sigmoid(a0_ref[dd:dd + 1, :] + jnp.dot(jnp.where(sel, ha, 0.0).astype(BF16), a2_ref[...],
                                                          preferred_element_type=F32))
        kd_ref[...] = k * (1.0 + (a - 1.0) * ka_ref[...])
        bb_ref[...] = kk * a


def rwkv_proj(xn, wts, latent, need_rg):
    b, t, d = xn.shape
    tr = min(PROJ_TILE, t)
    nt = t // tr
    halo = GRID_W
    rh = tr // halo
    nh = t // halo
    row = lambda bi, si: (bi, si, 0)
    n_out = 10 if need_rg else 8
    w_specs = [_full(w.shape) for w in wts]
    outs = pl.pallas_call(
        functools.partial(_rwkv_proj_kernel, latent=latent, need_rg=need_rg),
        grid=(b, nt),
        in_specs=[pl.BlockSpec((None, tr, d), row),
                  pl.BlockSpec((None, halo, d), lambda bi, si: (bi, jnp.maximum(si * rh - 1, 0), 0)),
                  pl.BlockSpec((None, halo, d), lambda bi, si: (bi, jnp.minimum((si + 1) * rh, nh - 1), 0))]
                 + w_specs,
        out_specs=[pl.BlockSpec((None, tr, d), row)] * n_out,
        out_shape=[jax.ShapeDtypeStruct((b, t, d), F32)] * n_out,
        scratch_shapes=[pltpu.VMEM((tr + 2 * halo, d), F32), pltpu.VMEM((tr, d), F32)],
        compiler_params=_cp(("parallel", "parallel")),
        name="rwkv_proj",
    )(xn, xn, xn, *wts)
    return outs


def _dot_nt(a, b):
    return lax.dot_general(a, b, (((1,), (1,)), ((), ())), preferred_element_type=F32)


def _wkv_pair_chunks(tiles, states, reverse, need_out):
    npair = len(states)
    c = tiles[0][0].shape[0]
    n2 = 2 * c
    each = lambda f, *ls: [f(*xs) for xs in zip(*ls)]
    dot = lambda a, b: jnp.dot(a, b, preferred_element_type=F32)
    bf = lambda z: z.astype(BF16)
    row = lax.broadcasted_iota(jnp.int32, (c, LANES), 0)
    head0 = lax.broadcasted_iota(jnp.int32, (c, LANES), 1) < RWKV_HEAD
    tr = lax.broadcasted_iota(jnp.int32, (n2, n2), 0)
    tc = lax.broadcasted_iota(jnp.int32, (n2, n2), 1)
    strict_of = {False: tr % c > tc % c, True: tr % c < tc % c}
    stricts = [strict_of[rv] for rv in reverse]
    ident = jnp.where(tr == tc, 1.0, 0.0)

    def stack(z):
        return bf(jnp.concatenate([jnp.where(head0, z, 0.0), jnp.where(head0, 0.0, z)], axis=0))

    def cumsum(lw, rv):
        cum = lw
        sh = 1
        while sh < c:
            if rv:
                cum = cum + jnp.where(row < c - sh, pltpu.roll(cum, c - sh, 0), 0.0)
            else:
                cum = cum + jnp.where(row >= sh, pltpu.roll(cum, sh, 0), 0.0)
            sh *= 2
        return cum

    al, v, lw, kd, bb = [[tl[i] for tl in tiles] for i in range(5)]
    cum = each(cumsum, lw, reverse)
    total = [cm[0:1, :] if rv else cm[c - 1:c, :] for cm, rv in zip(cum, reverse)]
    pinv = [jnp.exp(-cm) for cm in cum]
    a_st = each(lambda a, cm, l: stack(a * jnp.exp(cm - l)), al, cum, lw)
    b_st = each(lambda b, p: stack(b * p), bb, pinv)
    k_st = each(lambda k, p: stack(k * p), kd, pinv)
    v_st = each(stack, v)
    if need_out:
        rbar = each(lambda r, cm: r * jnp.exp(cm), [tl[5] for tl in tiles], cum)
        x_st = each(lambda a, r: jnp.concatenate([a, stack(r)], axis=0), a_st, rbar)
    else:
        x_st = a_st
    g = each(lambda xs, b, k: _dot_nt(xs, jnp.concatenate([b, k], axis=0)), x_st, b_st, k_st)
    mab = [jnp.where(st, gg[0:n2, 0:n2], 0.0) for gg, st in zip(g, stricts)]
    mv = each(lambda gg, vs, st: dot(bf(jnp.where(st, gg[0:n2, n2:2 * n2], 0.0)), vs), g, v_st, stricts)
    t = [ident + m for m in mab]
    y = mab
    power = 2
    while power < c:
        if power == 2:
            y = each(lambda z: dot(bf(z), bf(z)), y)
        if 2 * power < c:
            z2 = each(lambda tt, z: dot(bf(jnp.concatenate([tt, z], axis=0)), bf(z)), t, y)
            t = each(lambda tt, z: tt + z[0:n2], t, z2)
            y = [z[n2:2 * n2] for z in z2]
        else:
            t = each(lambda tt, z: tt + dot(bf(tt), bf(z)), t, y)
        power *= 2
    tb = each(bf, t)
    wu_st = each(lambda tt, a, m: dot(tt, jnp.concatenate([a, bf(m)], axis=1)), tb, a_st, mv)
    wm = [w[0:c, 0:LANES] + w[c:n2, 0:LANES] for w in wu_st]
    ut = [w[0:c, LANES:2 * LANES] + w[c:n2, LANES:2 * LANES] for w in wu_st]
    sb = each(bf, states)
    if need_out:
        incl_of = {rv: jnp.logical_or(m, tr % c == tc % c) for rv, m in strict_of.items()}
        incls = [incl_of[rv] for rv in reverse]
        arb = [bf(jnp.where(ic, gg[n2:2 * n2, 0:n2], 0.0)) for gg, ic in zip(g, incls)]
        ark = [bf(jnp.where(ic, gg[n2:2 * n2, n2:2 * n2], 0.0)) for gg, ic in zip(g, incls)]
        qy_st = each(lambda a, w: dot(a, bf(w)), arb, wu_st)
        qe_st = [q[:, 0:LANES] for q in qy_st]
        yi_st = each(lambda ak, vs, q: dot(ak, vs) + q[:, LANES:2 * LANES], ark, v_st, qy_st)
        qe = each(lambda r, q: r + q[0:c] + q[c:n2], rbar, qe_st)
        uy = each(lambda w, q, s: _dot_nt(bf(jnp.concatenate([w, q], axis=0)), s), wm, qe, sb)
        u = each(lambda z, ut_: z[0:c] + ut_, uy, ut)
        ys = each(lambda z, yi: z[c:n2] + yi[0:c] + yi[c:n2], uy, yi_st)
    else:
        u = each(lambda w, s, ut_: _dot_nt(bf(w), s) + ut_, wm, sb, ut)
        ys = [None] * npair
    same_head = (lax.broadcasted_iota(jnp.int32, (LANES, LANES), 0) // RWKV_HEAD
                 == lax.broadcasted_iota(jnp.int32, (LANES, LANES), 1) // RWKV_HEAD)

    def new_state(s, vv, uu, k, b, cm, tot):
        tail = jnp.exp(tot - cm)
        vu = bf(jnp.concatenate([vv, uu], axis=0))
        kb = bf(jnp.concatenate([k * tail, b * tail], axis=0))
        ds = lax.dot_general(vu, kb, (((0,), (0,)), ((), ())), preferred_element_type=F32)
        return s * jnp.exp(tot) + jnp.where(same_head, ds, 0.0)

    return ys, each(new_state, states, v, u, kd, bb, cum, total)


def _scan_kernel(*refs, need_out, has_init):
    it = iter(refs)
    dirs = []
    for _ in range(2):
        dirs.append([next(it) for _ in range(6 if need_out else 5)])
    s0_ref = next(it) if has_init else None
    y_refs = [next(it), next(it)] if need_out else None
    sfin_ref = next(it)
    s_ref = next(it)
    ti = pl.program_id(1)
    nb, _, d = dirs[0][0].shape
    nchunk = d // LANES

    @pl.when(ti == 0)
    def _():
        if has_init:
            s_ref[...] = s0_ref[...]
        else:
            s_ref[...] = jnp.zeros_like(s_ref)

    sls = [slice(c * LANES, (c + 1) * LANES) for c in range(nchunk)]

    def make_body(dir_list):
        slots = [(dd, c) for dd in dir_list for c in range(nchunk)]

        def per_batch(bidx, carry):
            tiles = [[ref[bidx, :, sls[c]] for ref in dirs[dd]] for dd, c in slots]
            states = [s_ref[dd, bidx, c] for dd, c in slots]
            ys, new_states = _wkv_pair_chunks(tiles, states, reverse=[dd == 1 for dd, _ in slots],
                                              need_out=need_out)
            for i, (dd, c) in enumerate(slots):
                s_ref[dd, bidx, c] = new_states[i]
                if need_out:
                    y_refs[dd][bidx, :, sls[c]] = ys[i]
            return carry

        return per_batch

    for dir_list in SCAN_DIR_GROUPS:
        lax.fori_loop(0, nb, make_body(dir_list), 0)

    @pl.when(ti == pl.num_programs(1) - 1)
    def _():
        sfin_ref[...] = s_ref[...]


def wkv_scan(nkk, v, lw, kd, bb, r, s0):
    b, t, d = v.shape
    tb = min(SCAN_CHUNK, t)
    nt = t // tb
    bg = math.gcd(b, SCAN_BATCH_GROUP)
    need_out = r is not None
    has_init = s0 is not None
    fwd = lambda g, i: (g, i, 0)
    bwd = lambda g, i: (g, nt - 1 - i, 0)
    args, in_specs = [], []
    for dd, imap in enumerate((fwd, bwd)):
        arrs = [nkk, v, lw[dd], kd[dd], bb[dd]] + ([r] if need_out else [])
        args += arrs
        in_specs += [pl.BlockSpec((bg, tb, d), imap)] * len(arrs)
    st_shape = (2, b, d // LANES, LANES, LANES)
    st_spec = pl.BlockSpec((2, bg, d // LANES, LANES, LANES), lambda g, i: (0, g, 0, 0, 0))
    if has_init:
        args.append(s0)
        in_specs.append(st_spec)
    out_specs, out_shape = [], []
    if need_out:
        out_specs += [pl.BlockSpec((bg, tb, d), fwd), pl.BlockSpec((bg, tb, d), bwd)]
        out_shape += [jax.ShapeDtypeStruct((b, t, d), F32)] * 2
    out_specs.append(st_spec)
    out_shape.append(jax.ShapeDtypeStruct(st_shape, F32))
    outs = pl.pallas_call(
        functools.partial(_scan_kernel, need_out=need_out, has_init=has_init),
        grid=(b // bg, nt),
        in_specs=in_specs,
        out_specs=out_specs,
        out_shape=out_shape,
        scratch_shapes=[pltpu.VMEM((2, bg, d // LANES, LANES, LANES), F32)],
        compiler_params=_cp(("arbitrary", "arbitrary")),
        name="wkv_scan",
    )(*args)
    if need_out:
        return outs[0], outs[1], outs[2]
    return None, None, outs[0]


def _readout_kernel(yf_ref, yb_ref, r_ref, v_ref, g_ref, kd0_ref, kd1_ref, rk_ref, gw_ref, gbias_ref,
                    wo_ref, h_ref, mods_ref, o_ref, z_ref):
    ones = _pair_ones()
    d = h_ref.shape[1]
    inv = 1.0 / RWKV_HEAD
    for c in range(d // LANES):
        sl = slice(c * LANES, (c + 1) * LANES)
        y = yf_ref[:, sl] + yb_ref[:, sl]
        mean = _segsum64(y, ones) * inv
        yc = y - mean
        var = _segsum64(yc * yc, ones) * inv
        yn = yc * lax.rsqrt(var + GN_EPS) * gw_ref[:, sl] + gbias_ref[:, sl]
        rkk = r_ref[:, sl] * (kd0_ref[:, sl] + kd1_ref[:, sl]) * rk_ref[:, sl]
        bonus = _segsum64(rkk, ones) * v_ref[:, sl]
        z_ref[:, sl] = ((yn + bonus) * g_ref[:, sl]).astype(BF16)
    out = jnp.dot(z_ref[...], wo_ref[...], preferred_element_type=F32)
    o_ref[...] = h_ref[...] + mods_ref[2:3, :] * out


def rwkv_readout(yf, yb, r, v, g, kd0, kd1, r_k, gn_w, gn_b, w_o, h, mods):
    b, t, d = h.shape
    tr = min(ROW_TILE, t)
    row = lambda bi, si: (bi, si, 0)
    rs = pl.BlockSpec((None, tr, d), row)
    return pl.pallas_call(
        _readout_kernel,
        grid=(b, t // tr),
        in_specs=[rs] * 7 + [_full((1, d))] * 3 + [_full((d, d)), rs,
                                                    pl.BlockSpec((None, 6, d), lambda bi, si: (bi, 0, 0))],
        out_specs=rs,
        out_shape=jax.ShapeDtypeStruct((b, t, d), F32),
        scratch_shapes=[pltpu.VMEM((tr, d), BF16)],
        compiler_params=_cp(("parallel", "parallel")),
        name="rwkv_readout",
    )(yf, yb, r, v, g, kd0, kd1, r_k.reshape(1, d), gn_w.reshape(1, d), gn_b.reshape(1, d), w_o, h, mods)


def _rope_tables(seq_len):
    t = jnp.arange(seq_len, dtype=jnp.int32)
    row = (t // GRID_W).astype(F32)
    col = (t % GRID_W).astype(F32)
    axis_dim = HEAD_DIM // 2
    inv_freq = ROPE_THETA ** (-jnp.arange(0, axis_dim, 2, dtype=F32) / axis_dim)
    ang = jnp.concatenate([row[:, None] * inv_freq, col[:, None] * inv_freq], axis=-1)
    cos = jnp.repeat(jnp.cos(ang), 2, axis=-1)
    sin = jnp.repeat(jnp.sin(ang), 2, axis=-1) * jnp.tile(jnp.array([-1.0, 1.0], F32), axis_dim)
    return jnp.tile(cos, (1, LANES // HEAD_DIM)), jnp.tile(sin, (1, LANES // HEAD_DIM))


def kernel(x, c, ctx, c_ctx, ada_w, ada_b, norm1_g, norm2_g, ev_w_in, ev_conv_w, ev_q_gain, ev_k_gain, ev_w_out,
           od_mu, od_w_r, od_w_k, od_w_v, od_w_o, od_g1, od_g2, od_k_k, od_k_a, od_r_k, od_w0, od_w1, od_w2,
           od_a0, od_a1, od_a2, od_gn_w, od_gn_b, peer_wq, peer_keys, peer_u, peer_v):
    b, s, d = x.shape
    lc = ctx.shape[1]
    depth = ada_w.shape[0]
    bf = lambda a: a.astype(BF16)

    pad_rows = (-(b + 1)) % 8
    cc = jnp.concatenate([c, c_ctx[None, :], jnp.zeros((pad_rows, d), F32)], axis=0)
    mods_all = ada_mods(cc, ada_w, ada_b)

    cos_l, sin_l = _rope_tables(s)
    cos_c = jnp.ones((lc, LANES), F32)
    sin_c = jnp.zeros((lc, LANES), F32)
    rep = LANES // HEAD_DIM

    hx, hc = x, ctx
    for i in range(depth):
        last = i == depth - 1
        j = i // 2
        mods_l = mods_all[i, :b].reshape(b, 6, d)
        mods_c = jnp.broadcast_to(mods_all[i, b].reshape(1, 6, d), (b, 6, d))
        if i % 2 == 0:
            w_in = bf(ev_w_in[j])
            w_out = bf(ev_w_out[j])
            qg = jnp.tile(ev_q_gain[j], rep).reshape(1, LANES)
            kg = jnp.tile(ev_k_gain[j], rep).reshape(1, LANES)
            u_l, gb_l, q_l, k_l, v_l = in_proj(hx, mods_l, norm1_g[i], w_in, qg, kg, cos_l, sin_l)
            u_c, gb_c, q_c, k_c, v_c = in_proj(hc, mods_c, norm1_g[i], w_in, qg, kg, cos_c, sin_c)
            att_l = attention(q_l, [(k_c, v_c), (k_l, v_l)])
            hx = conv_out(u_l, gb_l, att_l, ev_conv_w[j], w_out, hx, mods_l)
            if not last:
                att_c = attention(q_c, [(k_c, v_c)])
                hc = conv_out(u_c, gb_c, att_c, ev_conv_w[j], w_out, hc, mods_c)
        else:
            wts = [od_mu[j], bf(od_w_r[j]), bf(od_w_k[j]), bf(od_w_v[j]), bf(od_g1[j]), bf(od_g2[j]),
                   bf(jnp.concatenate([od_w1[j, 0], od_w1[j, 1]], axis=1)),
                   bf(jnp.concatenate([od_w2[j, 0], od_w2[j, 1]], axis=0)),
                   bf(jnp.concatenate([od_a1[j, 0], od_a1[j, 1]], axis=1)),
                   bf(jnp.concatenate([od_a2[j, 0], od_a2[j, 1]], axis=0)),
                   od_w0[j], od_a0[j], od_k_k[j].reshape(1, d), od_k_a[j].reshape(1, d)]
            xn_c = norm_mod(hc, mods_c, norm1_g[i])
            xn_l = norm_mod(hx, mods_l, norm1_g[i])
            pc = rwkv_proj(xn_c, wts, latent=False, need_rg=not last)
            pl_ = rwkv_proj(xn_l, wts, latent=True, need_rg=True)
            if last:
                v_c, nkk_c, d0c, d1c, kd0c, kd1c, bb0c, bb1c = pc
                r_c = g_c = None
            else:
                r_c, g_c, v_c, nkk_c, d0c, d1c, kd0c, kd1c, bb0c, bb1c = pc
            r_l, g_l, v_l, nkk_l, d0l, d1l, kd0l, kd1l, bb0l, bb1l = pl_
            yf_c, yb_c, s_ctx = wkv_scan(nkk_c, v_c, (d0c, d1c), (kd0c, kd1c), (bb0c, bb1c), r_c, None)
            yf_l, yb_l, _ = wkv_scan(nkk_l, v_l, (d0l, d1l), (kd0l, kd1l), (bb0l, bb1l), r_l, s_ctx)
            r_k = od_r_k[j].reshape(d)
            hx = rwkv_readout(yf_l, yb_l, r_l, v_l, g_l, kd0l, kd1l, r_k, od_gn_w[j], od_gn_b[j],
                              bf(od_w_o[j]), hx, mods_l)
            if not last:
                hc = rwkv_readout(yf_c, yb_c, r_c, v_c, g_c, kd0c, kd1c, r_k, od_gn_w[j], od_gn_b[j],
                                  bf(od_w_o[j]), hc, mods_c)
        wq = bf(peer_wq[i])
        keys = bf(peer_keys[i]).reshape(PEER_HEADS * 2, PEER_NKEYS, PEER_HALF)
        u_tab = bf(peer_u[i])
        v_tab = bf(peer_v[i])
        hx = peer_ffn_residual(hx, mods_l, norm2_g[i], wq, keys, u_tab, v_tab)
        if not last:
            hc = peer_ffn_residual(hc, mods_c, norm2_g[i], wq, keys, u_tab, v_tab)
    return hx
```

```python
import functools
import math

import jax
import jax.numpy as jnp
from jax import lax
from jax.experimental import pallas as pl
from jax.experimental.pallas import tpu as pltpu

F32 = jnp.float32
BF16 = jnp.bfloat16

GRID_W = 64
RMS_EPS = 1e-6
HEAD_DIM = 64
CONV_CH = 512
ATT_Q_HEADS = 8
ATT_KV_HEADS = 2
ATT_GROUP = ATT_Q_HEADS // ATT_KV_HEADS
ATT_SCALE = HEAD_DIM ** -0.5
ROPE_THETA = 10000.0
RWKV_HEAD = 64
GN_EPS = 64e-5
L2_EPS = 1e-12
PEER_HEADS = 8
PEER_NKEYS = 128
PEER_TOPK = 16
PEER_HALF = 128

LANES = 128
SUBLANES = 8
ROW_TILE = 256
ATT_TILE = 256
PROJ_TILE = 256
PEER_TOK_TILE = 256
PEER_EXP_CHUNK = 2048
PEER_SCATTER_UNROLL = 16
SCAN_CHUNK = 64
SCAN_BATCH_GROUP = 4
SCAN_DIR_GROUPS = ((0, 1),)
VMEM_LIMIT = 56 * 1024 * 1024


def _cp(sem, vmem=VMEM_LIMIT):
    return pltpu.CompilerParams(dimension_semantics=sem, vmem_limit_bytes=vmem)


def _full(shape):
    n = len(shape)
    return pl.BlockSpec(shape, lambda *_: (0,) * n)


def _pair_ones():
    r = lax.broadcasted_iota(jnp.int32, (LANES, LANES), 0) // HEAD_DIM
    c = lax.broadcasted_iota(jnp.int32, (LANES, LANES), 1) // HEAD_DIM
    return (r == c).astype(BF16)


def _segsum64(x, ones):
    hi = x.astype(BF16)
    lo = (x - hi.astype(F32)).astype(BF16)
    return (jnp.dot(hi, ones, preferred_element_type=F32)
            + jnp.dot(lo, ones, preferred_element_type=F32))


def _norm_mod(x, g, shift, scale):
    ms = jnp.mean(x * x, axis=-1, keepdims=True)
    return (x * lax.rsqrt(ms + RMS_EPS) * g) * (1.0 + scale) + shift


def _ada_kernel(c_ref, w_ref, b_ref, o_ref):
    c = c_ref[...]
    s = c * jax.nn.sigmoid(c)
    o_ref[...] = jnp.dot(s.astype(BF16), w_ref[...].astype(BF16), preferred_element_type=F32) + b_ref[...]


def ada_mods(cc, ada_w, ada_b):
    depth, d, d6 = ada_w.shape
    r = cc.shape[0]
    nblk = d6 // d
    return pl.pallas_call(
        _ada_kernel,
        grid=(depth, nblk),
        in_specs=[pl.BlockSpec((r, d), lambda i, j: (0, 0)),
                  pl.BlockSpec((None, d, d), lambda i, j: (i, 0, j)),
                  pl.BlockSpec((None, 1, d), lambda i, j: (i, 0, j))],
        out_specs=pl.BlockSpec((None, r, d), lambda i, j: (i, 0, j)),
        out_shape=jax.ShapeDtypeStruct((depth, r, d6), F32),
        compiler_params=_cp(("parallel", "parallel")),
        name="ada_mods",
    )(cc, ada_w, ada_b.reshape(depth, 1, d6))


def _inproj_kernel(h_ref, mods_ref, g_ref, w_ref, qg_ref, kg_ref, cos_ref, sin_ref,
                   u_ref, gb_ref, q_ref, k_ref, v_ref):
    ones = _pair_ones()
    x = h_ref[...]
    xn = _norm_mod(x, g_ref[...], mods_ref[0:1, :], mods_ref[1:2, :])
    y = jnp.dot(xn.astype(BF16), w_ref[...], preferred_element_type=F32)
    c = CONV_CH
    u_ref[...] = y[:, 2 * c:3 * c] * y[:, 0:c]
    gb_ref[...] = y[:, c:2 * c]
    cos = cos_ref[...]
    sin = sin_ref[...]
    lane = lax.broadcasted_iota(jnp.int32, (x.shape[0], LANES), 1)
    even = (lane % 2) == 0

    def qk_norm_rope(z, gain):
        ss = _segsum64(z * z, ones) * (1.0 / HEAD_DIM)
        zn = z * lax.rsqrt(ss + RMS_EPS) * gain
        partner = jnp.where(even, pltpu.roll(zn, LANES - 1, 1), pltpu.roll(zn, 1, 1))
        return zn * cos + partner * sin

    q0 = 3 * c
    for j in range(ATT_Q_HEADS * HEAD_DIM // LANES):
        z = y[:, q0 + j * LANES:q0 + (j + 1) * LANES]
        q_ref[:, j * LANES:(j + 1) * LANES] = qk_norm_rope(z, qg_ref[...]) * ATT_SCALE
    k0 = q0 + ATT_Q_HEADS * HEAD_DIM
    k_ref[...] = qk_norm_rope(y[:, k0:k0 + LANES], kg_ref[...]).astype(BF16)
    v_ref[...] = y[:, k0 + LANES:k0 + 2 * LANES].astype(BF16)


def in_proj(h, mods, g, w_in, q_gain, k_gain, cos_t, sin_t):
    b, t, d = h.shape
    tr = min(ROW_TILE, t)
    nt = t // tr
    proj = w_in.shape[1]
    row = lambda bi, si: (bi, si, 0)
    outs = pl.pallas_call(
        _inproj_kernel,
        grid=(b, nt),
        in_specs=[pl.BlockSpec((None, tr, d), row),
                  pl.BlockSpec((None, 6, d), lambda bi, si: (bi, 0, 0)),
                  _full((1, d)), _full((d, proj)), _full((1, LANES)), _full((1, LANES)),
                  pl.BlockSpec((tr, LANES), lambda bi, si: (si, 0)),
                  pl.BlockSpec((tr, LANES), lambda bi, si: (si, 0))],
        out_specs=[pl.BlockSpec((None, tr, CONV_CH), row),
                   pl.BlockSpec((None, tr, CONV_CH), row),
                   pl.BlockSpec((None, tr, CONV_CH), row),
                   pl.BlockSpec((None, tr, LANES), row),
                   pl.BlockSpec((None, tr, LANES), row)],
        out_shape=[jax.ShapeDtypeStruct((b, t, CONV_CH), F32),
                   jax.ShapeDtypeStruct((b, t, CONV_CH), F32),
                   jax.ShapeDtypeStruct((b, t, CONV_CH), F32),
                   jax.ShapeDtypeStruct((b, t, LANES), BF16),
                   jax.ShapeDtypeStruct((b, t, LANES), BF16)],
        compiler_params=_cp(("parallel", "parallel")),
        name="in_proj",
    )(h, mods, g.reshape(1, d), w_in, q_gain, k_gain, cos_t, sin_t)
    return outs


def _attn_kernel(*refs, n_sets):
    q_ref = refs[0]
    kv = refs[1:1 + 2 * n_sets]
    o_ref = refs[1 + 2 * n_sets]
    tq = q_ref.shape[0]
    lane = lax.broadcasted_iota(jnp.int32, (tq, LANES), 1)
    half = lane // HEAD_DIM
    nt = (((1,), (1,)), ((), ()))
    for cchunk in range(ATT_Q_HEADS * HEAD_DIM // LANES):
        qc = q_ref[:, cchunk * LANES:(cchunk + 1) * LANES]
        out_c = jnp.zeros((tq, LANES), F32)
        for hh in range(2):
            h = 2 * cchunk + hh
            j = h // ATT_GROUP
            qa = qc if hh == j else pltpu.roll(qc, HEAD_DIM, 1)
            qm = jnp.where(half == j, qa, 0.0).astype(BF16)
            ss = [lax.dot_general(qm, kv[2 * i][...], nt, preferred_element_type=F32)
                  for i in range(n_sets)]
            m = ss[0].max(axis=-1, keepdims=True)
            for s in ss[1:]:
                m = jnp.maximum(m, s.max(axis=-1, keepdims=True))
            l = jnp.zeros((tq, 1), F32)
            o = jnp.zeros((tq, LANES), F32)
            for i in range(n_sets):
                p = jnp.exp(ss[i] - m)
                l = l + p.sum(axis=-1, keepdims=True)
                o = o + jnp.dot(p.astype(BF16), kv[2 * i + 1][...], preferred_element_type=F32)
            o = o / l
            oa = o if hh == j else pltpu.roll(o, HEAD_DIM, 1)
            out_c = jnp.where(half == hh, oa, out_c)
        o_ref[:, cchunk * LANES:(cchunk + 1) * LANES] = out_c


def attention(q, kv_sets):
    b, t, qd = q.shape
    tq = min(ATT_TILE, t)
    in_specs = [pl.BlockSpec((None, tq, qd), lambda bi, si: (bi, si, 0))]
    args = [q]
    for k, v in kv_sets:
        ln = k.shape[1]
        in_specs += [pl.BlockSpec((None, ln, LANES), lambda bi, si: (bi, 0, 0))] * 2
        args += [k, v]
    return pl.pallas_call(
        functools.partial(_attn_kernel, n_sets=len(kv_sets)),
        grid=(b, t // tq),
        in_specs=in_specs,
        out_specs=pl.BlockSpec((None, tq, qd), lambda bi, si: (bi, si, 0)),
        out_shape=jax.ShapeDtypeStruct((b, t, qd), F32),
        compiler_params=_cp(("parallel", "parallel")),
        name="attention",
    )(*args)


def _conv_out_kernel(u_ref, up_ref, un_ref, gb_ref, att_ref, cw_ref, wo_ref, h_ref, mods_ref, o_ref):
    si = pl.program_id(1)
    last = pl.num_programs(1) - 1
    u = u_ref[...]
    tr = u.shape[0]
    row = lax.broadcasted_iota(jnp.int32, u.shape, 0)
    prev_row = jnp.where(si == 0, 0.0, up_ref[7:8, :])
    next_row = jnp.where(si == last, 0.0, un_ref[0:1, :])
    u_m1 = jnp.where(row == 0, prev_row, pltpu.roll(u, 1, 0))
    u_p1 = jnp.where(row == tr - 1, next_row, pltpu.roll(u, tr - 1, 0))
    conv = u_m1 * cw_ref[0:1, :] + u * cw_ref[1:2, :] + u_p1 * cw_ref[2:3, :]
    conv = gb_ref[...] * conv
    c = CONV_CH
    y = (jnp.dot(conv.astype(BF16), wo_ref[0:c, :], preferred_element_type=F32)
         + jnp.dot(att_ref[...].astype(BF16), wo_ref[c:2 * c, :], preferred_element_type=F32))
    o_ref[...] = h_ref[...] + mods_ref[2:3, :] * y


def conv_out(u, gb, att, conv_w, w_out, h, mods):
    b, t, d = h.shape
    tr = min(ROW_TILE, t)
    nt = t // tr
    r8 = tr // 8
    n8 = t // 8
    row = lambda bi, si: (bi, si, 0)
    return pl.pallas_call(
        _conv_out_kernel,
        grid=(b, nt),
        in_specs=[pl.BlockSpec((None, tr, CONV_CH), row),
                  pl.BlockSpec((None, 8, CONV_CH), lambda bi, si: (bi, jnp.maximum(si * r8 - 1, 0), 0)),
                  pl.BlockSpec((None, 8, CONV_CH), lambda bi, si: (bi, jnp.minimum((si + 1) * r8, n8 - 1), 0)),
                  pl.BlockSpec((None, tr, CONV_CH), row),
                  pl.BlockSpec((None, tr, CONV_CH), row),
                  _full((3, CONV_CH)), _full((d, d)),
                  pl.BlockSpec((None, tr, d), row),
                  pl.BlockSpec((None, 6, d), lambda bi, si: (bi, 0, 0))],
        out_specs=pl.BlockSpec((None, tr, d), row),
        out_shape=jax.ShapeDtypeStruct((b, t, d), F32),
        compiler_params=_cp(("parallel", "parallel")),
        name="conv_out",
    )(u, u, u, gb, att, conv_w, w_out, h, mods)


def _topk_rows(s, k, pos=None):
    n, t = s.shape
    if pos is None:
        pos = lax.broadcasted_iota(jnp.int32, (n, t), 0).astype(F32)
    krow = lax.broadcasted_iota(jnp.int32, (k, t), 0)
    vals = jnp.zeros((k, t), F32)
    idxs = jnp.zeros((k, t), F32)
    for it in range(k):
        m = jnp.max(s, axis=0, keepdims=True)
        idx = jnp.min(jnp.where(s == m, pos, jnp.inf), axis=0, keepdims=True)
        s = jnp.where(pos == idx, -jnp.inf, s)
        vals = jnp.where(krow == it, m, vals)
        idxs = jnp.where(krow == it, idx, idxs)
    return vals, idxs


PEER_NCAND = PEER_TOPK + (PEER_TOPK // 2 - 1) * SUBLANES + PEER_TOPK // 2


def _router_kernel(h_ref, mods_ref, g_ref, wq_ref, keys_ref, xn_ref, a_ref, b_ref, gate_ref,
                   cand_ref, res_ref):
    x = h_ref[...]
    t = x.shape[0]
    xn = _norm_mod(x, g_ref[...], mods_ref[3:4, :], mods_ref[4:5, :])
    xb = xn.astype(BF16)
    xn_ref[...] = xb
    q = jnp.dot(xb, wq_ref[...], preferred_element_type=F32).astype(BF16)
    nt = (((1,), (1,)), ((), ()))
    kk = PEER_TOPK
    j8 = lax.broadcasted_iota(jnp.int32, (SUBLANES, t), 0)
    j8f = j8.astype(F32)
    cand_pos = jnp.concatenate(
        [lax.broadcasted_iota(jnp.int32, (kk, t), 0).astype(F32)]
        + [j8f + float(i * kk) for i in range(1, kk // 2)]
        + [(j8f + float(kk // 2)) * float(kk)], axis=0)
    ncand = cand_pos.shape[0]
    for h in range(PEER_HEADS):
        tops = []
        for p in range(2):
            g = 2 * h + p
            st = lax.dot_general(keys_ref[g], q[:, g * PEER_HALF:(g + 1) * PEER_HALF], nt,
                                 preferred_element_type=F32)
            tops.append(_topk_rows(st, kk))
        (s1, i1), (s2, i2) = tops
        cand_ref[0:kk, :] = s1[0:1, :] + s2
        for i in range(1, kk // 2):
            r0 = kk + (i - 1) * SUBLANES
            cand_ref[r0:r0 + SUBLANES, :] = jnp.where(j8 < kk // (i + 1), s1[i:i + 1, :] + s2[0:SUBLANES, :],
                                                      -jnp.inf)
        cand_ref[ncand - kk // 2:ncand, :] = s1[kk // 2:kk, :] + s2[0:1, :]
        top_s, pos = _topk_rows(cand_ref[...], kk, cand_pos)
        p1 = jnp.floor(pos * (1.0 / kk))
        p2 = pos - p1 * kk
        a_idx = jnp.zeros((kk, t), F32)
        b_idx = jnp.zeros((kk, t), F32)
        for k1 in range(kk):
            a_idx = a_idx + jnp.where(p1 == k1, i1[k1:k1 + 1, :], 0.0)
            b_idx = b_idx + jnp.where(p2 == k1, i2[k1:k1 + 1, :], 0.0)
        e = jnp.exp(top_s - top_s[0:1, :])
        gate = e / jnp.sum(e, axis=0, keepdims=True)
        res_ref[0, h * kk:(h + 1) * kk, :] = a_idx
        res_ref[1, h * kk:(h + 1) * kk, :] = b_idx
        res_ref[2, h * kk:(h + 1) * kk, :] = gate
    a_ref[...] = res_ref[0].T.astype(jnp.int32)
    b_ref[...] = res_ref[1].T.astype(jnp.int32)
    gate_ref[...] = res_ref[2].T


def peer_router(h, mods, g, wq, keys):
    b, t, d = h.shape
    tr = min(ROW_TILE, t)
    row = lambda bi, si: (bi, si, 0)
    hk = PEER_HEADS * PEER_TOPK
    return pl.pallas_call(
        _router_kernel,
        grid=(b, t // tr),
        in_specs=[pl.BlockSpec((None, tr, d), row),
                  pl.BlockSpec((None, 6, d), lambda bi, si: (bi, 0, 0)),
                  _full((1, d)), _full(wq.shape), _full(keys.shape)],
        out_specs=[pl.BlockSpec((None, tr, d), row),
                   pl.BlockSpec((None, tr, hk), row),
                   pl.BlockSpec((None, tr, hk), row),
                   pl.BlockSpec((None, tr, hk), row)],
        out_shape=[jax.ShapeDtypeStruct((b, t, d), BF16),
                   jax.ShapeDtypeStruct((b, t, hk), jnp.int32),
                   jax.ShapeDtypeStruct((b, t, hk), jnp.int32),
                   jax.ShapeDtypeStruct((b, t, hk), F32)],
        scratch_shapes=[pltpu.VMEM((PEER_NCAND, tr), F32),
                        pltpu.VMEM((3, hk, tr), F32)],
        compiler_params=_cp(("parallel", "parallel")),
        name="peer_router",
    )(h, mods, g.reshape(1, d), wq, keys)


def _peer_dense_kernel(xn_ref, a_ref, b_ref, gate_ref, u_ref, v_ref, h_ref, mods_ref, o_ref,
                       g_ref, acc_ref, w_ref):
    j = pl.program_id(1)
    tt = xn_ref.shape[0]
    nk = PEER_NKEYS
    nt = (((1,), (1,)), ((), ()))

    @pl.when(j == 0)
    def _():
        acc_ref[...] = jnp.zeros_like(acc_ref)
        sub = lax.broadcasted_iota(jnp.int32, (nk, a_ref.shape[1]), 0)

        def per_groups(i0, carry):
            for gi in range(PEER_SCATTER_UNROLL):
                i = i0 * PEER_SCATTER_UNROLL + gi
                rows = pl.ds(pl.multiple_of(i * SUBLANES, SUBLANES), SUBLANES)
                ai8, bi8, gt8 = a_ref[rows, :], b_ref[rows, :], gate_ref[rows, :]
                for k in range(SUBLANES):
                    at = jnp.where(sub == ai8[k:k + 1, :], gt8[k:k + 1, :], 0.0).astype(BF16)
                    bt = (sub == bi8[k:k + 1, :]).astype(BF16)
                    g_ref[i, pl.ds(k, nk, stride=SUBLANES), :] = lax.dot_general(
                        at, bt, nt, preferred_element_type=F32)
            return carry

        lax.fori_loop(0, tt // (SUBLANES * PEER_SCATTER_UNROLL), per_groups, 0)

    ec = u_ref.shape[0]
    halves = [slice(p * (ec // 2), (p + 1) * (ec // 2)) for p in range(2)]
    xb = xn_ref[...]
    hmats = [lax.dot_general(xb, u_ref[sl, :], nt, preferred_element_type=F32) for sl in halves]
    nblk = ec // 2 // nk
    acc = acc_ref[...]
    for p, sl in enumerate(halves):
        for blk in range(nblk):
            hb = hmats[p][:, blk * nk:(blk + 1) * nk]
            act = 0.5 * hb * (1.0 + lax.erf(hb * (1.0 / math.sqrt(2.0))))
            a0 = (j * 2 + p) * nblk + blk
            gate = g_ref[:, pl.ds(pl.multiple_of(a0 * SUBLANES, SUBLANES), SUBLANES), :].reshape(tt, nk)
            w_ref[:, sl.start + blk * nk:sl.start + (blk + 1) * nk] = (gate * act).astype(BF16)
        acc = acc + jnp.dot(w_ref[:, sl], v_ref[sl, :], preferred_element_type=F32)
    acc_ref[...] = acc

    @pl.when(j == pl.num_programs(1) - 1)
    def _():
        o_ref[...] = h_ref[...] + mods_ref[5:6, :] * acc_ref[...]


def peer_dense(xn, a_idx, b_idx, gate, u_tab, v_tab, h, mods):
    b, t, d = h.shape
    n = b * t
    tt = min(PEER_TOK_TILE, t)
    tiles_per_seq = t // tt
    ne = u_tab.shape[0]
    ec = PEER_EXP_CHUNK
    hk = a_idx.shape[-1]
    tok = lambda i, j: (i, 0)
    out = pl.pallas_call(
        _peer_dense_kernel,
        grid=(n // tt, ne // ec),
        in_specs=[pl.BlockSpec((tt, d), tok),
                  pl.BlockSpec((tt, hk), tok), pl.BlockSpec((tt, hk), tok), pl.BlockSpec((tt, hk), tok),
                  pl.BlockSpec((ec, d), lambda i, j: (j, 0)),
                  pl.BlockSpec((ec, d), lambda i, j: (j, 0)),
                  pl.BlockSpec((tt, d), tok),
                  pl.BlockSpec((None, 6, d), lambda i, j: (i // tiles_per_seq, 0, 0))],
        out_specs=pl.BlockSpec((tt, d), tok),
        out_shape=jax.ShapeDtypeStruct((n, d), F32),
        scratch_shapes=[pltpu.VMEM((tt // SUBLANES, PEER_NKEYS * SUBLANES, PEER_NKEYS), F32),
                        pltpu.VMEM((tt, d), F32),
                        pltpu.VMEM((tt, ec), BF16)],
        compiler_params=_cp(("parallel", "arbitrary")),
        name="peer_dense",
    )(xn.reshape(n, d), a_idx.reshape(n, hk), b_idx.reshape(n, hk), gate.reshape(n, hk),
      u_tab, v_tab, h.reshape(n, d), mods)
    return out.reshape(b, t, d)


def peer_ffn_residual(h, mods, g, wq, keys, u_tab, v_tab):
    xn, a_idx, b_idx, gate = peer_router(h, mods, g, wq, keys)
    return peer_dense(xn, a_idx, b_idx, gate, u_tab, v_tab, h, mods)


def _normmod_kernel(h_ref, mods_ref, g_ref, o_ref):
    o_ref[...] = _norm_mod(h_ref[...], g_ref[...], mods_ref[0:1, :], mods_ref[1:2, :])


def norm_mod(h, mods, g):
    b, t, d = h.shape
    tr = min(ROW_TILE, t)
    row = lambda bi, si: (bi, si, 0)
    return pl.pallas_call(
        _normmod_kernel,
        grid=(b, t // tr),
        in_specs=[pl.BlockSpec((None, tr, d), row),
                  pl.BlockSpec((None, 6, d), lambda bi, si: (bi, 0, 0)),
                  _full((1, d))],
        out_specs=pl.BlockSpec((None, tr, d), row),
        out_shape=jax.ShapeDtypeStruct((b, t, d), F32),
        compiler_params=_cp(("parallel", "parallel")),
        name="norm_mod",
    )(h, mods, g.reshape(1, d))


def _rwkv_proj_kernel(x_ref, xp_ref, xnx_ref, mu_ref, wr_ref, wk_ref, wv_ref, g1_ref, g2_ref,
                      w1_ref, w2_ref, a1_ref, a2_ref, w0_ref, a0_ref, kk_ref, ka_ref,
                      *out_and_scratch, latent, need_rg):
    if need_rg:
        (r_ref, g_ref, v_ref, nkk_ref, dec0_ref, dec1_ref, kd0_ref, kd1_ref, bb0_ref, bb1_ref,
         ext_ref, sh_ref) = out_and_scratch
    else:
        (v_ref, nkk_ref, dec0_ref, dec1_ref, kd0_ref, kd1_ref, bb0_ref, bb1_ref,
         ext_ref, sh_ref) = out_and_scratch
    si = pl.program_id(1)
    last = pl.num_programs(1) - 1
    halo = xp_ref.shape[0]
    t, d = x_ref.shape
    x = x_ref[...]
    ext_ref[0:halo, :] = jnp.where(si == 0, 0.0, xp_ref[...])
    ext_ref[halo:halo + t, :] = x
    ext_ref[halo + t:halo + t + halo, :] = jnp.where(si == last, 0.0, xnx_ref[...])
    if latent:
        q = d // 4
        col = lax.broadcasted_iota(jnp.int32, (t, q), 0) % GRID_W
        sh_ref[:, 0:q] = jnp.where(col == 0, 0.0, ext_ref[halo - 1:halo - 1 + t, 0:q])
        sh_ref[:, q:2 * q] = jnp.where(col == GRID_W - 1, 0.0, ext_ref[halo + 1:halo + 1 + t, q:2 * q])
        sh_ref[:, 2 * q:3 * q] = ext_ref[halo - GRID_W:halo - GRID_W + t, 2 * q:3 * q]
        sh_ref[:, 3 * q:4 * q] = ext_ref[halo + GRID_W:halo + GRID_W + t, 3 * q:4 * q]
    else:
        hd = d // 2
        sh_ref[:, 0:hd] = ext_ref[halo - 1:halo - 1 + t, 0:hd]
        sh_ref[:, hd:d] = ext_ref[halo + 1:halo + 1 + t, hd:d]
    xx = sh_ref[...] - x

    def mix(m):
        return (x + xx * mu_ref[m:m + 1, :]).astype(BF16)

    ones = _pair_ones()
    lane = lax.broadcasted_iota(jnp.int32, (t, LANES), 1)
    k = jnp.dot(mix(2), wk_ref[...], preferred_element_type=F32)
    v_ref[...] = jnp.dot(mix(3), wv_ref[...], preferred_element_type=F32)
    if need_rg:
        r_ref[...] = jnp.dot(mix(0), wr_ref[...], preferred_element_type=F32)
        gg = jax.nn.sigmoid(jnp.dot(mix(5), g1_ref[...], preferred_element_type=F32))
        g_ref[...] = jnp.dot(gg.astype(BF16), g2_ref[...], preferred_element_type=F32)
    hw = jnp.tanh(jnp.dot(mix(1), w1_ref[...], preferred_element_type=F32))
    ha = jnp.dot(mix(4), a1_ref[...], preferred_element_type=F32)
    for c in range(d // LANES):
        sl = slice(c * LANES, (c + 1) * LANES)
        kc = k[:, sl] * kk_ref[:, sl]
        nrm = _segsum64(kc * kc, ones)
        kc = kc * lax.rsqrt(nrm + L2_EPS)
        nkk_ref[:, sl] = -kc
    kk = -nkk_ref[...]
    for dd, (dec_ref, kd_ref, bb_ref) in enumerate(((dec0_ref, kd0_ref, bb0_ref), (dec1_ref, kd1_ref, bb1_ref))):
        sel = (lane // (LANES // 2)) == dd
        z = w0_ref[dd:dd + 1, :] + jnp.dot(jnp.where(sel, hw, 0.0).astype(BF16), w2_ref[...],
                                            preferred_element_type=F32)
        nz = -z
        softplus = jnp.maximum(nz, 0.0) + jnp.log(1.0 + jnp.exp(-jnp.abs(nz)))
        logw = -softplus - 0.5
        dec_ref[...] = -jnp.exp(logw)
        a = jax.nn.sigmoid(a0_ref[dd:dd + 1, :] + jnp.dot(jnp.where(sel, ha, 0.0).astype(BF16), a2_ref[...],
                                                          preferred_element_type=F32))
        kd_ref[...] = k * (1.0 + (a - 1.0) * ka_ref[...])
        bb_ref[...] = kk * a


def rwkv_proj(xn, wts, latent, need_rg):
    b, t, d = xn.shape
    tr = min(PROJ_TILE, t)
    nt = t // tr
    halo = GRID_W
    rh = tr // halo
    nh = t // halo
    row = lambda bi, si: (bi, si, 0)
    n_out = 10 if need_rg else 8
    w_specs = [_full(w.shape) for w in wts]
    outs = pl.pallas_call(
        functools.partial(_rwkv_proj_kernel, latent=latent, need_rg=need_rg),
        grid=(b, nt),
        in_specs=[pl.BlockSpec((None, tr, d), row),
                  pl.BlockSpec((None, halo, d), lambda bi, si: (bi, jnp.maximum(si * rh - 1, 0), 0)),
                  pl.BlockSpec((None, halo, d), lambda bi, si: (bi, jnp.minimum((si + 1) * rh, nh - 1), 0))]
                 + w_specs,
        out_specs=[pl.BlockSpec((None, tr, d), row)] * n_out,
        out_shape=[jax.ShapeDtypeStruct((b, t, d), F32)] * n_out,
        scratch_shapes=[pltpu.VMEM((tr + 2 * halo, d), F32), pltpu.VMEM((tr, d), F32)],
        compiler_params=_cp(("parallel", "parallel")),
        name="rwkv_proj",
    )(xn, xn, xn, *wts)
    return outs


def _dot_nt(a, b):
    return lax.dot_general(a, b, (((1,), (1,)), ((), ())), preferred_element_type=F32)


def _wkv_pair_chunks(tiles, states, reverse, need_out):
    npair = len(states)
    c = tiles[0][0].shape[0]
    n2 = 2 * c
    each = lambda f, *ls: [f(*xs) for xs in zip(*ls)]
    dot = lambda a, b: jnp.dot(a, b, preferred_element_type=F32)
    bf = lambda z: z.astype(BF16)
    row = lax.broadcasted_iota(jnp.int32, (c, LANES), 0)
    head0 = lax.broadcasted_iota(jnp.int32, (c, LANES), 1) < RWKV_HEAD
    tr = lax.broadcasted_iota(jnp.int32, (n2, n2), 0)
    tc = lax.broadcasted_iota(jnp.int32, (n2, n2), 1)
    strict_of = {False: tr % c > tc % c, True: tr % c < tc % c}
    stricts = [strict_of[rv] for rv in reverse]
    ident = jnp.where(tr == tc, 1.0, 0.0)

    def stack(z):
        return bf(jnp.concatenate([jnp.where(head0, z, 0.0), jnp.where(head0, 0.0, z)], axis=0))

    def cumsum(lw, rv):
        cum = lw
        sh = 1
        while sh < c:
            if rv:
                cum = cum + jnp.where(row < c - sh, pltpu.roll(cum, c - sh, 0), 0.0)
            else:
                cum = cum + jnp.where(row >= sh, pltpu.roll(cum, sh, 0), 0.0)
            sh *= 2
        return cum

    al, v, lw, kd, bb = [[tl[i] for tl in tiles] for i in range(5)]
    cum = each(cumsum, lw, reverse)
    total = [cm[0:1, :] if rv else cm[c - 1:c, :] for cm, rv in zip(cum, reverse)]
    pinv = [jnp.exp(-cm) for cm in cum]
    a_st = each(lambda a, cm, l: stack(a * jnp.exp(cm - l)), al, cum, lw)
    b_st = each(lambda b, p: stack(b * p), bb, pinv)
    k_st = each(lambda k, p: stack(k * p), kd, pinv)
    v_st = each(stack, v)
    if need_out:
        rbar = each(lambda r, cm: r * jnp.exp(cm), [tl[5] for tl in tiles], cum)
        x_st = each(lambda a, r: jnp.concatenate([a, stack(r)], axis=0), a_st, rbar)
    else:
        x_st = a_st
    g = each(lambda xs, b, k: _dot_nt(xs, jnp.concatenate([b, k], axis=0)), x_st, b_st, k_st)
    mab = [jnp.where(st, gg[0:n2, 0:n2], 0.0) for gg, st in zip(g, stricts)]
    mv = each(lambda gg, vs, st: dot(bf(jnp.where(st, gg[0:n2, n2:2 * n2], 0.0)), vs), g, v_st, stricts)
    t = [ident + m for m in mab]
    y = mab
    power = 2
    while power < c:
        if power == 2:
            y = each(lambda z: dot(bf(z), bf(z)), y)
        if 2 * power < c:
            z2 = each(lambda tt, z: dot(bf(jnp.concatenate([tt, z], axis=0)), bf(z)), t, y)
            t = each(lambda tt, z: tt + z[0:n2], t, z2)
            y = [z[n2:2 * n2] for z in z2]
        else:
            t = each(lambda tt, z: tt + dot(bf(tt), bf(z)), t, y)
        power *= 2
    tb = each(bf, t)
    wu_st = each(lambda tt, a, m: dot(tt, jnp.concatenate([a, bf(m)], axis=1)), tb, a_st, mv)
    wm = [w[0:c, 0:LANES] + w[c:n2, 0:LANES] for w in wu_st]
    ut = [w[0:c, LANES:2 * LANES] + w[c:n2, LANES:2 * LANES] for w in wu_st]
    sb = each(bf, states)
    if need_out:
        incl_of = {rv: jnp.logical_or(m, tr % c == tc % c) for rv, m in strict_of.items()}
        incls = [incl_of[rv] for rv in reverse]
        arb = [bf(jnp.where(ic, gg[n2:2 * n2, 0:n2], 0.0)) for gg, ic in zip(g, incls)]
        ark = [bf(jnp.where(ic, gg[n2:2 * n2, n2:2 * n2], 0.0)) for gg, ic in zip(g, incls)]
        qy_st = each(lambda a, w: dot(a, bf(w)), arb, wu_st)
        qe_st = [q[:, 0:LANES] for q in qy_st]
        yi_st = each(lambda ak, vs, q: dot(ak, vs) + q[:, LANES:2 * LANES], ark, v_st, qy_st)
        qe = each(lambda r, q: r + q[0:c] + q[c:n2], rbar, qe_st)
        uy = each(lambda w, q, s: _dot_nt(bf(jnp.concatenate([w, q], axis=0)), s), wm, qe, sb)
        u = each(lambda z, ut_: z[0:c] + ut_, uy, ut)
        ys = each(lambda z, yi: z[c:n2] + yi[0:c] + yi[c:n2], uy, yi_st)
    else:
        u = each(lambda w, s, ut_: _dot_nt(bf(w), s) + ut_, wm, sb, ut)
        ys = [None] * npair
    same_head = (lax.broadcasted_iota(jnp.int32, (LANES, LANES), 0) // RWKV_HEAD
                 == lax.broadcasted_iota(jnp.int32, (LANES, LANES), 1) // RWKV_HEAD)

    def new_state(s, vv, uu, k, b, cm, tot):
        tail = jnp.exp(tot - cm)
        vu = bf(jnp.concatenate([vv, uu], axis=0))
        kb = bf(jnp.concatenate([k * tail, b * tail], axis=0))
        ds = lax.dot_general(vu, kb, (((0,), (0,)), ((), ())), preferred_element_type=F32)
        return s * jnp.exp(tot) + jnp.where(same_head, ds, 0.0)

    return ys, each(new_state, states, v, u, kd, bb, cum, total)


def _scan_kernel(*refs, need_out, has_init):
    it = iter(refs)
    dirs = []
    for _ in range(2):
        dirs.append([next(it) for _ in range(6 if need_out else 5)])
    s0_ref = next(it) if has_init else None
    y_refs = [next(it), next(it)] if need_out else None
    sfin_ref = next(it)
    s_ref = next(it)
    ti = pl.program_id(1)
    nb, _, d = dirs[0][0].shape
    nchunk = d // LANES

    @pl.when(ti == 0)
    def _():
        if has_init:
            s_ref[...] = s0_ref[...]
        else:
            s_ref[...] = jnp.zeros_like(s_ref)

    sls = [slice(c * LANES, (c + 1) * LANES) for c in range(nchunk)]

    def make_body(dir_list):
        slots = [(dd, c) for dd in dir_list for c in range(nchunk)]

        def per_batch(bidx, carry):
            tiles = [[ref[bidx, :, sls[c]] for ref in dirs[dd]] for dd, c in slots]
            states = [s_ref[dd, bidx, c] for dd, c in slots]
            ys, new_states = _wkv_pair_chunks(tiles, states, reverse=[dd == 1 for dd, _ in slots],
                                              need_out=need_out)
            for i, (dd, c) in enumerate(slots):
                s_ref[dd, bidx, c] = new_states[i]
                if need_out:
                    y_refs[dd][bidx, :, sls[c]] = ys[i]
            return carry

        return per_batch

    for dir_list in SCAN_DIR_GROUPS:
        lax.fori_loop(0, nb, make_body(dir_list), 0)

    @pl.when(ti == pl.num_programs(1) - 1)
    def _():
        sfin_ref[...] = s_ref[...]


def wkv_scan(nkk, v, lw, kd, bb, r, s0):
    b, t, d = v.shape
    tb = min(SCAN_CHUNK, t)
    nt = t // tb
    bg = math.gcd(b, SCAN_BATCH_GROUP)
    need_out = r is not None
    has_init = s0 is not None
    fwd = lambda g, i: (g, i, 0)
    bwd = lambda g, i: (g, nt - 1 - i, 0)
    args, in_specs = [], []
    for dd, imap in enumerate((fwd, bwd)):
        arrs = [nkk, v, lw[dd], kd[dd], bb[dd]] + ([r] if need_out else [])
        args += arrs
        in_specs += [pl.BlockSpec((bg, tb, d), imap)] * len(arrs)
    st_shape = (2, b, d // LANES, LANES, LANES)
    st_spec = pl.BlockSpec((2, bg, d // LANES, LANES, LANES), lambda g, i: (0, g, 0, 0, 0))
    if has_init:
        args.append(s0)
        in_specs.append(st_spec)
    out_specs, out_shape = [], []
    if need_out:
        out_specs += [pl.BlockSpec((bg, tb, d), fwd), pl.BlockSpec((bg, tb, d), bwd)]
        out_shape += [jax.ShapeDtypeStruct((b, t, d), F32)] * 2
    out_specs.append(st_spec)
    out_shape.append(jax.ShapeDtypeStruct(st_shape, F32))
    outs = pl.pallas_call(
        functools.partial(_scan_kernel, need_out=need_out, has_init=has_init),
        grid=(b // bg, nt),
        in_specs=in_specs,
        out_specs=out_specs,
        out_shape=out_shape,
        scratch_shapes=[pltpu.VMEM((2, bg, d // LANES, LANES, LANES), F32)],
        compiler_params=_cp(("arbitrary", "arbitrary")),
        name="wkv_scan",
    )(*args)
    if need_out:
        return outs[0], outs[1], outs[2]
    return None, None, outs[0]


def _readout_kernel(yf_ref, yb_ref, r_ref, v_ref, g_ref, kd0_ref, kd1_ref, rk_ref, gw_ref, gbias_ref,
                    wo_ref, h_ref, mods_ref, o_ref, z_ref):
    ones = _pair_ones()
    d = h_ref.shape[1]
    inv = 1.0 / RWKV_HEAD
    for c in range(d // LANES):
        sl = slice(c * LANES, (c + 1) * LANES)
        y = yf_ref[:, sl] + yb_ref[:, sl]
        mean = _segsum64(y, ones) * inv
        yc = y - mean
        var = _segsum64(yc * yc, ones) * inv
        yn = yc * lax.rsqrt(var + GN_EPS) * gw_ref[:, sl] + gbias_ref[:, sl]
        rkk = r_ref[:, sl] * (kd0_ref[:, sl] + kd1_ref[:, sl]) * rk_ref[:, sl]
        bonus = _segsum64(rkk, ones) * v_ref[:, sl]
        z_ref[:, sl] = ((yn + bonus) * g_ref[:, sl]).astype(BF16)
    out = jnp.dot(z_ref[...], wo_ref[...], preferred_element_type=F32)
    o_ref[...] = h_ref[...] + mods_ref[2:3, :] * out


def rwkv_readout(yf, yb, r, v, g, kd0, kd1, r_k, gn_w, gn_b, w_o, h, mods):
    b, t, d = h.shape
    tr = min(ROW_TILE, t)
    row = lambda bi, si: (bi, si, 0)
    rs = pl.BlockSpec((None, tr, d), row)
    return pl.pallas_call(
        _readout_kernel,
        grid=(b, t // tr),
        in_specs=[rs] * 7 + [_full((1, d))] * 3 + [_full((d, d)), rs,
                                                    pl.BlockSpec((None, 6, d), lambda bi, si: (bi, 0, 0))],
        out_specs=rs,
        out_shape=jax.ShapeDtypeStruct((b, t, d), F32),
        scratch_shapes=[pltpu.VMEM((tr, d), BF16)],
        compiler_params=_cp(("parallel", "parallel")),
        name="rwkv_readout",
    )(yf, yb, r, v, g, kd0, kd1, r_k.reshape(1, d), gn_w.reshape(1, d), gn_b.reshape(1, d), w_o, h, mods)


def _rope_tables(seq_len):
    t = jnp.arange(seq_len, dtype=jnp.int32)
    row = (t // GRID_W).astype(F32)
    col = (t % GRID_W).astype(F32)
    axis_dim = HEAD_DIM // 2
    inv_freq = ROPE_THETA ** (-jnp.arange(0, axis_dim, 2, dtype=F32) / axis_dim)
    ang = jnp.concatenate([row[:, None] * inv_freq, col[:, None] * inv_freq], axis=-1)
    cos = jnp.repeat(jnp.cos(ang), 2, axis=-1)
    sin = jnp.repeat(jnp.sin(ang), 2, axis=-1) * jnp.tile(jnp.array([-1.0, 1.0], F32), axis_dim)
    return jnp.tile(cos, (1, LANES // HEAD_DIM)), jnp.tile(sin, (1, LANES // HEAD_DIM))


def kernel(x, c, ctx, c_ctx, ada_w, ada_b, norm1_g, norm2_g, ev_w_in, ev_conv_w, ev_q_gain, ev_k_gain, ev_w_out,
           od_mu, od_w_r, od_w_k, od_w_v, od_w_o, od_g1, od_g2, od_k_k, od_k_a, od_r_k, od_w0, od_w1, od_w2,
           od_a0, od_a1, od_a2, od_gn_w, od_gn_b, peer_wq, peer_keys, peer_u, peer_v):
    b, s, d = x.shape
    lc = ctx.shape[1]
    depth = ada_w.shape[0]
    bf = lambda a: a.astype(BF16)

    pad_rows = (-(b + 1)) % 8
    cc = jnp.concatenate([c, c_ctx[None, :], jnp.zeros((pad_rows, d), F32)], axis=0)
    mods_all = ada_mods(cc, ada_w, ada_b)

    cos_l, sin_l = _rope_tables(s)
    cos_c = jnp.ones((lc, LANES), F32)
    sin_c = jnp.zeros((lc, LANES), F32)
    rep = LANES // HEAD_DIM

    hx, hc = x, ctx
    for i in range(depth):
        last = i == depth - 1
        j = i // 2
        mods_l = mods_all[i, :b].reshape(b, 6, d)
        mods_c = jnp.broadcast_to(mods_all[i, b].reshape(1, 6, d), (b, 6, d))
        if i % 2 == 0:
            w_in = bf(ev_w_in[j])
            w_out = bf(ev_w_out[j])
            qg = jnp.tile(ev_q_gain[j], rep).reshape(1, LANES)
            kg = jnp.tile(ev_k_gain[j], rep).reshape(1, LANES)
            u_l, gb_l, q_l, k_l, v_l = in_proj(hx, mods_l, norm1_g[i], w_in, qg, kg, cos_l, sin_l)
            u_c, gb_c, q_c, k_c, v_c = in_proj(hc, mods_c, norm1_g[i], w_in, qg, kg, cos_c, sin_c)
            att_l = attention(q_l, [(k_c, v_c), (k_l, v_l)])
            hx = conv_out(u_l, gb_l, att_l, ev_conv_w[j], w_out, hx, mods_l)
            if not last:
                att_c = attention(q_c, [(k_c, v_c)])
                hc = conv_out(u_c, gb_c, att_c, ev_conv_w[j], w_out, hc, mods_c)
        else:
            wts = [od_mu[j], bf(od_w_r[j]), bf(od_w_k[j]), bf(od_w_v[j]), bf(od_g1[j]), bf(od_g2[j]),
                   bf(jnp.concatenate([od_w1[j, 0], od_w1[j, 1]], axis=1)),
                   bf(jnp.concatenate([od_w2[j, 0], od_w2[j, 1]], axis=0)),
                   bf(jnp.concatenate([od_a1[j, 0], od_a1[j, 1]], axis=1)),
                   bf(jnp.concatenate([od_a2[j, 0], od_a2[j, 1]], axis=0)),
                   od_w0[j], od_a0[j], od_k_k[j].reshape(1, d), od_k_a[j].reshape(1, d)]
            xn_c = norm_mod(hc, mods_c, norm1_g[i])
            xn_l = norm_mod(hx, mods_l, norm1_g[i])
            pc = rwkv_proj(xn_c, wts, latent=False, need_rg=not last)
            pl_ = rwkv_proj(xn_l, wts, latent=True, need_rg=True)
            if last:
                v_c, nkk_c, d0c, d1c, kd0c, kd1c, bb0c, bb1c = pc
                r_c = g_c = None
            else:
                r_c, g_c, v_c, nkk_c, d0c, d1c, kd0c, kd1c, bb0c, bb1c = pc
            r_l, g_l, v_l, nkk_l, d0l, d1l, kd0l, kd1l, bb0l, bb1l = pl_
            yf_c, yb_c, s_ctx = wkv_scan(nkk_c, v_c, (d0c, d1c), (kd0c, kd1c), (bb0c, bb1c), r_c, None)
            yf_l, yb_l, _ = wkv_scan(nkk_l, v_l, (d0l, d1l), (kd0l, kd1l), (bb0l, bb1l), r_l, s_ctx)
            r_k = od_r_k[j].reshape(d)
            hx = rwkv_readout(yf_l, yb_l, r_l, v_l, g_l, kd0l, kd1l, r_k, od_gn_w[j], od_gn_b[j],
                              bf(od_w_o[j]), hx, mods_l)
            if not last:
                hc = rwkv_readout(yf_c, yb_c, r_c, v_c, g_c, kd0c, kd1c, r_k, od_gn_w[j], od_gn_b[j],
                                  bf(od_w_o[j]), hc, mods_c)
        wq = bf(peer_wq[i])
        keys = bf(peer_keys[i]).reshape(PEER_HEADS * 2, PEER_NKEYS, PEER_HALF)
        u_tab = bf(peer_u[i])
        v_tab = bf(peer_v[i])
        hx = peer_ffn_residual(hx, mods_l, norm2_g[i], wq, keys, u_tab, v_tab)
        if not last:
            hc = peer_ffn_residual(hc, mods_c, norm2_g[i], wq, keys, u_tab, v_tab)
    return hx
```
